```python
import jax, jax.numpy as jnp
from jax import lax
import numpy as np

D_MODEL = 2048
BATCH = 16
SEQ = 2048
DEPTH = 1

N_META = 16
BLK = 128
META_PAD = BLK - N_META
A_HEADS = 16
A_KV_HEADS = 4
A_HEAD_DIM = 64
A_WINDOW = 128
B_HEADS = 8
B_Q_RANK = 512
B_KV_RANK = 256
B_NOPE = 128
B_ROPE = 64
B_V = 128
ROPE_THETA = 10000.0
A_Q_W = A_HEADS * A_HEAD_DIM
A_KV_W = A_KV_HEADS * A_HEAD_DIM
IN_W = A_Q_W + 2 * A_KV_W + B_Q_RANK + B_KV_RANK + B_ROPE
SPLIT_POINTS = [A_Q_W, A_Q_W + A_KV_W, A_Q_W + 2 * A_KV_W, A_Q_W + 2 * A_KV_W + B_Q_RANK, A_Q_W + 2 * A_KV_W + B_Q_RANK + B_KV_RANK]
D_MIX = A_HEADS * A_HEAD_DIM + B_HEADS * B_V
N_EXPERTS = 256
N_GROUPS = 8
TOPK_GROUPS = 4
TOP_K = 8
D_EXPERT = 512
D_SHARED = 512
ROUTED_SCALE = 2.5
MOE_BLOCK = 256
DN_ALPHA = (2 * DEPTH) ** 0.25
DN_BETA = (8 * DEPTH) ** -0.25
LN_EPS = 1e-5
RMS_EPS = 1e-6
NEG_INF = -1e30

kernel_name = 'hymba_style_swa_mla_moe_deepnorm_encoder'


def layer_norm(x, g, b):
    x32 = x.astype(jnp.float32)
    mu = jnp.mean(x32, axis=-1, keepdims=True)
    var = jnp.mean(jnp.square(x32 - mu), axis=-1, keepdims=True)
    y = (x32 - mu) * lax.rsqrt(var + LN_EPS) * g.astype(jnp.float32) + b.astype(jnp.float32)
    return y.astype(x.dtype)


def rms_norm(x, g):
    x32 = x.astype(jnp.float32)
    y = x32 * lax.rsqrt(jnp.mean(jnp.square(x32), axis=-1, keepdims=True) + RMS_EPS) * g.astype(jnp.float32)
    return y.astype(x.dtype)


def rope(x, cos, sin):
    half = x.shape[-1] // 2
    x1, x2 = x[..., :half], x[..., half:]
    cos = cos.astype(x.dtype)
    sin = sin.astype(x.dtype)
    return jnp.concatenate([x1 * cos - x2 * sin, x2 * cos + x1 * sin], axis=-1)


def alibi_slopes(n_heads):
    return jnp.exp2(-8.0 * jnp.arange(1, n_heads + 1, dtype=jnp.float32) / n_heads)


def windowed_gqa(q, k, v, sink):
    b, lp = q.shape[0], q.shape[1]
    nb = lp // BLK
    grp = A_HEADS // A_KV_HEADS
    band_pad = [(0, 0), (BLK, BLK), (0, 0), (0, 0)]
    kp = jnp.pad(k, band_pad)
    vp = jnp.pad(v, band_pad)
    qb = jnp.moveaxis(q.reshape(b, nb, BLK, A_KV_HEADS, grp, A_HEAD_DIM), 1, 0)
    slopes = alibi_slopes(A_HEADS).reshape(A_KV_HEADS, grp)
    sink_r = sink.astype(jnp.float32).reshape(A_KV_HEADS, grp)
    scale = A_HEAD_DIM ** -0.5

    def one_block(args):
        i, qi = args
        kw = lax.dynamic_slice_in_dim(kp, i * BLK, 3 * BLK, axis=1)
        vw = lax.dynamic_slice_in_dim(vp, i * BLK, 3 * BLK, axis=1)
        qidx = i * BLK + jnp.arange(BLK)
        kidx = i * BLK - BLK + jnp.arange(3 * BLK)
        dist = jnp.abs(qidx[:, None] - kidx[None, :])
        valid = (dist <= A_WINDOW) & (kidx >= META_PAD)[None, :] & (kidx < lp)[None, :]
        s = jnp.einsum('bqhgd,bshd->bhgqs', qi, kw, preferred_element_type=jnp.float32) * scale
        s = s - slopes[None, :, :, None, None] * dist.astype(jnp.float32)[None, None, None]
        s = jnp.where(valid[None, None, None], s, NEG_INF)
        sk = jnp.broadcast_to(sink_r[None, :, :, None, None], s.shape[:-1] + (1,))
        p = jax.nn.softmax(jnp.concatenate([s, sk], axis=-1), axis=-1)[..., :-1]
        o = jnp.einsum('bhgqs,bshd->bqhgd', p.astype(vw.dtype), vw)
        return o.reshape(b, BLK, A_HEADS * A_HEAD_DIM)

    out = lax.map(one_block, (jnp.arange(nb), qb))
    return jnp.moveaxis(out, 0, 1).reshape(b, lp, A_HEADS * A_HEAD_DIM)


def mla_attention(cq, ckv, kpe, q_norm_g, w_q_up, kv_norm_g, w_kv_up, cos, sin, key_valid):
    b, lp = cq.shape[0], cq.shape[1]
    nb = lp // BLK
    q = jnp.einsum('blr,rhe->blhe', rms_norm(cq, q_norm_g), w_q_up)
    q_nope = q[..., :B_NOPE]
    q_pe = rope(q[..., B_NOPE:], cos[:, None, :], sin[:, None, :])
    kv = jnp.einsum('blr,rhe->blhe', rms_norm(ckv, kv_norm_g), w_kv_up)
    k_nope = kv[..., :B_NOPE]
    v = kv[..., B_NOPE:]
    k_pe = rope(kpe, cos, sin)
    scale = (B_NOPE + B_ROPE) ** -0.5
    qn_b = jnp.moveaxis(q_nope.reshape(b, nb, BLK, B_HEADS, B_NOPE), 1, 0)
    qp_b = jnp.moveaxis(q_pe.reshape(b, nb, BLK, B_HEADS, B_ROPE), 1, 0)

    def one_block(args):
        qn, qp = args
        s = (jnp.einsum('bqhe,bshe->bhqs', qn, k_nope, preferred_element_type=jnp.float32)
             + jnp.einsum('bqhe,bse->bhqs', qp, k_pe, preferred_element_type=jnp.float32)) * scale
        s = jnp.where(key_valid[None, None, None, :], s, NEG_INF)
        p = jax.nn.softmax(s, axis=-1)
        o = jnp.einsum('bhqs,bshe->bqhe', p.astype(v.dtype), v)
        return o.reshape(b, BLK, B_HEADS * B_V)

    out = lax.map(one_block, (qn_b, qp_b))
    return jnp.moveaxis(out, 0, 1).reshape(b, lp, B_HEADS * B_V)


def moe_ffn(h, w_router, router_bias, w_gate, w_up, w_down, w_sh_gate, w_sh_up, w_sh_down):
    n, d = h.shape
    scores = jax.nn.sigmoid(jnp.dot(h.astype(jnp.float32), w_router.astype(jnp.float32)))
    sel = scores + router_bias.astype(jnp.float32)
    grp_score = lax.top_k(sel.reshape(n, N_GROUPS, N_EXPERTS // N_GROUPS), 2)[0].sum(-1)
    _, gidx = lax.top_k(grp_score, TOPK_GROUPS)
    gmask = jnp.any(gidx[:, :, None] == jnp.arange(N_GROUPS)[None, None, :], axis=1)
    sel = jnp.where(jnp.repeat(gmask, N_EXPERTS // N_GROUPS, axis=1), sel, NEG_INF)
    _, eidx = lax.top_k(sel, TOP_K)
    wts = jnp.take_along_axis(scores, eidx, axis=1)
    wts = wts / jnp.sum(wts, axis=-1, keepdims=True) * ROUTED_SCALE

    nk = n * TOP_K
    flat_e = eidx.reshape(-1)
    flat_t = jnp.arange(nk, dtype=jnp.int32) // TOP_K
    flat_w = wts.reshape(-1)
    order = jnp.argsort(flat_e)
    se = flat_e[order]
    counts = jnp.zeros((N_EXPERTS,), jnp.int32).at[flat_e].add(1)
    starts = jnp.cumsum(counts) - counts
    padded = (counts + MOE_BLOCK - 1) // MOE_BLOCK * MOE_BLOCK
    pad_ends = jnp.cumsum(padded)
    pad_starts = pad_ends - padded
    dest = pad_starts[se] + jnp.arange(nk, dtype=jnp.int32) - starts[se]
    n_rows = (nk + MOE_BLOCK - 1) // MOE_BLOCK * MOE_BLOCK + N_EXPERTS * MOE_BLOCK
    n_blocks = n_rows // MOE_BLOCK
    row_tok = jnp.full((n_rows,), n, jnp.int32).at[dest].set(flat_t[order])
    row_w = jnp.zeros((n_rows,), jnp.float32).at[dest].set(flat_w[order])
    block_e = jnp.minimum(jnp.searchsorted(pad_ends, jnp.arange(n_blocks) * MOE_BLOCK, side='right'), N_EXPERTS - 1)
    h_pad = jnp.concatenate([h, jnp.zeros((1, d), h.dtype)], axis=0)

    def body(acc, blk):
        tok, w, e = blk
        xb = h_pad[tok]
        y = (jax.nn.silu(xb @ w_gate[e]) * (xb @ w_up[e])) @ w_down[e]
        return acc.at[tok].add(y * w[:, None].astype(y.dtype)), None

    acc, _ = lax.scan(body, jnp.zeros((n + 1, d), h.dtype),
                      (row_tok.reshape(n_blocks, MOE_BLOCK), row_w.reshape(n_blocks, MOE_BLOCK), block_e))
    shared = (jax.nn.silu(h @ w_sh_gate) * (h @ w_sh_up)) @ w_sh_down
    return acc[:n] + shared


def setup_inputs(seed: int = 0) -> dict:
    key = jax.random.key(seed)
    ks = jax.random.split(key, 24)
    f32 = jnp.float32
    nrm = lambda k, shape, s: jax.random.normal(k, shape, f32) * s
    return {
        'x': nrm(ks[0], (BATCH, SEQ, D_MODEL), 1.0),
        'meta_tokens': nrm(ks[1], (N_META, D_MODEL), 1.0),
        'ln_emb_g': 1.0 + nrm(ks[2], (D_MODEL,), 0.02),
        'ln_emb_b': nrm(ks[3], (D_MODEL,), 0.02),
        'w_in': nrm(ks[4], (DEPTH, D_MODEL, IN_W), D_MODEL ** -0.5),
        'q_norm_g': 1.0 + nrm(ks[5], (DEPTH, B_Q_RANK), 0.02),
        'w_q_up': nrm(ks[6], (DEPTH, B_Q_RANK, B_HEADS, B_NOPE + B_ROPE), B_Q_RANK ** -0.5),
        'kv_norm_g': 1.0 + nrm(ks[7], (DEPTH, B_KV_RANK), 0.02),
        'w_kv_up': nrm(ks[8], (DEPTH, B_KV_RANK, B_HEADS, B_NOPE + B_V), B_KV_RANK ** -0.5),
        'sink_logits': nrm(ks[9], (DEPTH, A_HEADS), 0.5),
        'w_out': nrm(ks[10], (DEPTH, D_MIX, D_MODEL), D_MIX ** -0.5 * DN_BETA),
        'ln_mix_g': 1.0 + nrm(ks[11], (DEPTH, D_MODEL), 0.02),
        'ln_mix_b': nrm(ks[12], (DEPTH, D_MODEL), 0.02),
        'w_router': nrm(ks[13], (DEPTH, D_MODEL, N_EXPERTS), D_MODEL ** -0.5),
        'router_bias': nrm(ks[14], (DEPTH, N_EXPERTS), 0.01),
        'w_exp_gate': nrm(ks[15], (DEPTH, N_EXPERTS, D_MODEL, D_EXPERT), D_MODEL ** -0.5),
        'w_exp_up': nrm(ks[16], (DEPTH, N_EXPERTS, D_MODEL, D_EXPERT), D_MODEL ** -0.5),
        'w_exp_down': nrm(ks[17], (DEPTH, N_EXPERTS, D_EXPERT, D_MODEL), D_EXPERT ** -0.5 * DN_BETA),
        'w_sh_gate': nrm(ks[18], (DEPTH, D_MODEL, D_SHARED), D_MODEL ** -0.5),
        'w_sh_up': nrm(ks[19], (DEPTH, D_MODEL, D_SHARED), D_MODEL ** -0.5),
        'w_sh_down': nrm(ks[20], (DEPTH, D_SHARED, D_MODEL), D_SHARED ** -0.5 * DN_BETA),
        'ln_ffn_g': 1.0 + nrm(ks[21], (DEPTH, D_MODEL), 0.02),
        'ln_ffn_b': nrm(ks[22], (DEPTH, D_MODEL), 0.02),
    }


def reference(x, meta_tokens, ln_emb_g, ln_emb_b, w_in, q_norm_g, w_q_up, kv_norm_g, w_kv_up,
              sink_logits, w_out, ln_mix_g, ln_mix_b, w_router, router_bias, w_exp_gate, w_exp_up,
              w_exp_down, w_sh_gate, w_sh_up, w_sh_down, ln_ffn_g, ln_ffn_b):
    b, s, d = x.shape
    lp = META_PAD + N_META + s
    meta = jnp.broadcast_to(meta_tokens[None].astype(x.dtype), (b, N_META, d))
    h = jnp.concatenate([jnp.zeros((b, META_PAD, d), x.dtype), meta, x], axis=1)
    h = layer_norm(h, ln_emb_g, ln_emb_b)

    idx = jnp.arange(lp)
    pos = (idx - META_PAD).astype(jnp.float32)
    inv_freq = ROPE_THETA ** (-jnp.arange(0, B_ROPE, 2, dtype=jnp.float32) / B_ROPE)
    ang = pos[:, None] * inv_freq[None, :]
    cos, sin = jnp.cos(ang), jnp.sin(ang)
    key_valid = idx >= META_PAD

    for l in range(DEPTH):
        proj = jnp.einsum('bld,de->ble', h, w_in[l])
        qa, ka, va, cq, ckv, kpe = jnp.split(proj, SPLIT_POINTS, axis=-1)
        oa = windowed_gqa(qa.reshape(b, lp, A_HEADS, A_HEAD_DIM),
                          ka.reshape(b, lp, A_KV_HEADS, A_HEAD_DIM),
                          va.reshape(b, lp, A_KV_HEADS, A_HEAD_DIM), sink_logits[l])
        ob = mla_attention(cq, ckv, kpe, q_norm_g[l], w_q_up[l], kv_norm_g[l], w_kv_up[l], cos, sin, key_valid)
        mix = jnp.concatenate([oa, ob], axis=-1) @ w_out[l]
        h = layer_norm(DN_ALPHA * h + mix, ln_mix_g[l], ln_mix_b[l])
        if l == DEPTH - 1:
            h = h[:, META_PAD + N_META:]
        hb, hl = h.shape[0], h.shape[1]
        f = moe_ffn(h.reshape(hb * hl, d), w_router[l], router_bias[l], w_exp_gate[l], w_exp_up[l],
                    w_exp_down[l], w_sh_gate[l], w_sh_up[l], w_sh_down[l]).reshape(hb, hl, d)
        h = layer_norm(DN_ALPHA * h + f, ln_ffn_g[l], ln_ffn_b[l])
    return h
```

```python
import functools

import jax
import jax.numpy as jnp
from jax import lax
from jax.experimental import pallas as pl
from jax.experimental.pallas import tpu as pltpu

F32 = jnp.float32
BF16 = jnp.bfloat16
U32 = jnp.uint32
I32 = jnp.int32

D_MODEL = 2048
N_META = 16
BLK = 128
A_HEADS = 16
A_KV_HEADS = 4
A_HEAD_DIM = 64
A_WINDOW = 128
B_HEADS = 8
B_Q_RANK = 512
B_KV_RANK = 256
B_NOPE = 128
B_ROPE = 64
B_V = 128
ROPE_THETA = 10000.0
A_Q_W = A_HEADS * A_HEAD_DIM
A_KV_W = A_KV_HEADS * A_HEAD_DIM
N_EXPERTS = 256
N_GROUPS = 8
GROUP_SIZE = N_EXPERTS // N_GROUPS
TOPK_GROUPS = 4
TOP_K = 8
D_EXPERT = 512
D_SHARED = 512
ROUTED_SCALE = 2.5
MOE_BLOCK = 256
DEPTH = 1
DN_ALPHA = (2 * DEPTH) ** 0.25
LN_EPS = 1e-5
RMS_EPS = 1e-6
NEG_INF = -1e30
HALF = D_MODEL // 2

A_SCALE = A_HEAD_DIM ** -0.5
B_SCALE = (B_NOPE + B_ROPE) ** -0.5
B_QK = B_NOPE + B_ROPE

SUBLANES = 8

VMEM_LIMIT = 56 * 1024 * 1024

NT_DIMS = (((1,), (1,)), ((), ()))


def _const_spec(shape):
    zeros = (0,) * len(shape)
    return pl.BlockSpec(shape, lambda *_: zeros, pipeline_mode=pl.Buffered(1))


def _pack_halves(v):
    bits = lax.bitcast_convert_type(v, U32)
    half = v.shape[1] // 2
    return (bits[:, :half] >> 16) | (bits[:, half:] & jnp.uint32(0xFFFF0000))


def _unpack_halves(u):
    lo = lax.bitcast_convert_type(u << 16, F32)
    hi = lax.bitcast_convert_type(u & jnp.uint32(0xFFFF0000), F32)
    return lo, hi


def _layer_norm(z, g, b):
    mu = jnp.mean(z, axis=-1, keepdims=True)
    zc = z - mu
    var = jnp.mean(zc * zc, axis=-1, keepdims=True)
    return zc * lax.rsqrt(var + LN_EPS) * g + b


def _rms_norm(z, g):
    return z * lax.rsqrt(jnp.mean(z * z, axis=-1, keepdims=True) + RMS_EPS) * g


def _silu(z):
    return z * jax.nn.sigmoid(z)


def _embed_proj_kernel(x_ref, g_ref, b_ref, win_ref, gq_ref, wq_ref, gkv_ref, wkv_ref, cos_ref, sin_ref,
                       h0_ref, qa_ref, ka_ref, va_ref, qm_ref, km_ref, vm_ref):
    h = _layer_norm(x_ref[...], g_ref[...], b_ref[...])
    hb = h.astype(BF16)
    h0_ref[...] = hb
    p = jnp.dot(hb, win_ref[...], preferred_element_type=F32)
    o = 0
    qa_ref[...] = (p[:, o:o + A_Q_W] * A_SCALE).astype(BF16)
    o += A_Q_W
    ka_ref[...] = p[:, o:o + A_KV_W].astype(BF16)
    o += A_KV_W
    va_ref[...] = p[:, o:o + A_KV_W].astype(BF16)
    o += A_KV_W
    cq = p[:, o:o + B_Q_RANK]
    o += B_Q_RANK
    ckv = p[:, o:o + B_KV_RANK]
    o += B_KV_RANK
    kpe = p[:, o:o + B_ROPE]
    o += B_ROPE
    kpe_rot = p[:, o:o + B_ROPE]

    cos = cos_ref[...]
    sin = sin_ref[...]
    qall = jnp.dot(_rms_norm(cq, gq_ref[...]).astype(BF16), wq_ref[...], preferred_element_type=F32)
    n_nope = B_HEADS * B_NOPE
    n_pe = B_HEADS * B_ROPE
    q_pe = qall[:, n_nope:n_nope + n_pe] * cos + qall[:, n_nope + n_pe:] * sin
    kv = jnp.dot(_rms_norm(ckv, gkv_ref[...]).astype(BF16), wkv_ref[...], preferred_element_type=F32)
    k_pe = (kpe * cos[:, :B_ROPE] + kpe_rot * sin[:, :B_ROPE]).astype(BF16)
    for hd in range(B_HEADS):
        qn = (qall[:, hd * B_NOPE:(hd + 1) * B_NOPE] * B_SCALE).astype(BF16)
        qp = (q_pe[:, hd * B_ROPE:(hd + 1) * B_ROPE] * B_SCALE).astype(BF16)
        qm_ref[hd] = jnp.concatenate([qn, qp], axis=1)
        kn = kv[:, hd * B_NOPE:(hd + 1) * B_NOPE].astype(BF16)
        km_ref[hd] = jnp.concatenate([kn, k_pe], axis=1)
        vm_ref[hd] = kv[:, n_nope + hd * B_V:n_nope + (hd + 1) * B_V].astype(BF16)


def _embed_proj(x2, batch, seq, tm, ln_g, ln_b, w_in, gq, wq, gkv, wkv, cos, sin):
    n = batch * seq
    nt = seq // tm
    row = lambda b, i: (b * nt + i, 0)
    head_major = lambda b, i: (b, 0, i, 0)
    in_w = w_in.shape[1]
    return pl.pallas_call(
        _embed_proj_kernel,
        grid=(batch, nt),
        in_specs=[
            pl.BlockSpec((tm, D_MODEL), row),
            _const_spec((1, D_MODEL)), _const_spec((1, D_MODEL)),
            _const_spec((D_MODEL, in_w)),
            _const_spec((1, B_Q_RANK)), _const_spec(wq.shape),
            _const_spec((1, B_KV_RANK)), _const_spec(wkv.shape),
            pl.BlockSpec((tm, B_HEADS * B_ROPE), lambda b, i: (i, 0)),
            pl.BlockSpec((tm, B_HEADS * B_ROPE), lambda b, i: (i, 0)),
        ],
        out_specs=[
            pl.BlockSpec((tm, D_MODEL), row),
            pl.BlockSpec((tm, A_Q_W), row),
            pl.BlockSpec((tm, A_KV_W), row),
            pl.BlockSpec((tm, A_KV_W), row),
            pl.BlockSpec((None, B_HEADS, tm, B_QK), head_major),
            pl.BlockSpec((None, B_HEADS, tm, B_QK), head_major),
            pl.BlockSpec((None, B_HEADS, tm, B_V), head_major),
        ],
        out_shape=[
            jax.ShapeDtypeStruct((n, D_MODEL), BF16),
            jax.ShapeDtypeStruct((n, A_Q_W), BF16),
            jax.ShapeDtypeStruct((n, A_KV_W), BF16),
            jax.ShapeDtypeStruct((n, A_KV_W), BF16),
            jax.ShapeDtypeStruct((batch, B_HEADS, seq, B_QK), BF16),
            jax.ShapeDtypeStruct((batch, B_HEADS, seq, B_QK), BF16),
            jax.ShapeDtypeStruct((batch, B_HEADS, seq, B_V), BF16),
        ],
        compiler_params=pltpu.CompilerParams(
            dimension_semantics=("arbitrary", "arbitrary"), vmem_limit_bytes=VMEM_LIMIT),
        name="embed_proj",
    )(x2, ln_g, ln_b, w_in, gq, wq, gkv, wkv, cos, sin)


def _win_attn_kernel(sink_ref, q_ref, kp_ref, kc_ref, kn_ref, vp_ref, vc_ref, vn_ref, kmeta_ref, vmeta_ref,
                     o_ref):
    i = pl.program_id(1)
    first = i == 0
    last = i == pl.num_programs(1) - 1
    k3 = jnp.concatenate([jnp.where(first, kmeta_ref[...], kp_ref[...]), kc_ref[...], kn_ref[...]], axis=0)
    v3 = jnp.concatenate([jnp.where(first, vmeta_ref[...], vp_ref[...]), vc_ref[...], vn_ref[...]], axis=0)
    qi = lax.broadcasted_iota(I32, (BLK, 3 * BLK), 0)
    ki = lax.broadcasted_iota(I32, (BLK, 3 * BLK), 1)
    dist = jnp.abs(BLK + qi - ki)
    k_lo = jnp.where(first, BLK - N_META, 0)
    k_hi = jnp.where(last, 2 * BLK, 3 * BLK)
    valid = (dist <= A_WINDOW) & (ki >= k_lo) & (ki < k_hi)
    dist_f = dist.astype(F32)
    grp = A_HEADS // A_KV_HEADS
    outs = []
    for hd in range(A_HEADS):
        kvh = hd // grp
        slope = 2.0 ** (-8.0 * (hd + 1) / A_HEADS)
        q = q_ref[:, hd * A_HEAD_DIM:(hd + 1) * A_HEAD_DIM]
        k = k3[:, kvh * A_HEAD_DIM:(kvh + 1) * A_HEAD_DIM]
        v = v3[:, kvh * A_HEAD_DIM:(kvh + 1) * A_HEAD_DIM]
        s = lax.dot_general(q, k, NT_DIMS, preferred_element_type=F32)
        s = jnp.where(valid, s - slope * dist_f, NEG_INF)
        sink = sink_ref[hd]
        m = jnp.maximum(jnp.max(s, axis=-1, keepdims=True), sink)
        p = jnp.exp(s - m)
        denom = jnp.sum(p, axis=-1, keepdims=True) + jnp.exp(sink - m)
        o = jnp.dot(p.astype(BF16), v, preferred_element_type=F32)
        outs.append(o / denom)
    o_ref[...] = jnp.concatenate(outs, axis=1).astype(BF16)


def _win_attn(sink, qa, ka, va, kmeta, vmeta, batch, seq):
    nb = seq // BLK
    n = batch * seq
    cur = lambda b, i, *_: (b * nb + i, 0)
    prev = lambda b, i, *_: (b * nb + jnp.maximum(i - 1, 0), 0)
    nxt = lambda b, i, *_: (b * nb + jnp.minimum(i + 1, nb - 1), 0)
    kv_spec = lambda f: pl.BlockSpec((BLK, A_KV_W), f)
    const = lambda b, i, *_: (0, 0)
    return pl.pallas_call(
        _win_attn_kernel,
        grid_spec=pltpu.PrefetchScalarGridSpec(
            num_scalar_prefetch=1,
            grid=(batch, nb),
            in_specs=[
                pl.BlockSpec((BLK, A_Q_W), cur),
                kv_spec(prev), kv_spec(cur), kv_spec(nxt),
                kv_spec(prev), kv_spec(cur), kv_spec(nxt),
                pl.BlockSpec((BLK, A_KV_W), const), pl.BlockSpec((BLK, A_KV_W), const),
            ],
            out_specs=pl.BlockSpec((BLK, A_Q_W), cur),
        ),
        out_shape=jax.ShapeDtypeStruct((n, A_Q_W), BF16),
        compiler_params=pltpu.CompilerParams(
            dimension_semantics=("arbitrary", "arbitrary"), vmem_limit_bytes=VMEM_LIMIT),
        name="win_attn",
    )(sink, qa, ka, ka, ka, va, va, va, kmeta, vmeta)


def _mla_attn_kernel(q_ref, k_ref, v_ref, km_ref, vm_ref, o_ref):
    q = q_ref[...]
    s = lax.dot_general(q, k_ref[...], NT_DIMS, preferred_element_type=F32)
    sm = lax.dot_general(q, km_ref[...], NT_DIMS, preferred_element_type=F32)
    m = jnp.maximum(jnp.max(s, axis=-1, keepdims=True), jnp.max(sm, axis=-1, keepdims=True))
    p = jnp.exp(s - m)
    pm = jnp.exp(sm - m)
    denom = jnp.sum(p, axis=-1, keepdims=True) + jnp.sum(pm, axis=-1, keepdims=True)
    o = jnp.dot(p.astype(BF16), v_ref[...], preferred_element_type=F32)
    o = o + jnp.dot(pm.astype(BF16), vm_ref[...], preferred_element_type=F32)
    o_ref[...] = (o / denom).astype(BF16)


def _mla_attn(qm, km, vm, km_meta, vm_meta, tq):
    batch, heads, seq, _ = qm.shape
    nq = seq // tq
    return pl.pallas_call(
        _mla_attn_kernel,
        grid=(batch, heads, nq),
        in_specs=[
            pl.BlockSpec((None, None, tq, B_QK), lambda b, h, i: (b, h, i, 0)),
            pl.BlockSpec((None, None, seq, B_QK), lambda b, h, i: (b, h, 0, 0)),
            pl.BlockSpec((None, None, seq, B_V), lambda b, h, i: (b, h, 0, 0)),
            pl.BlockSpec((None, None, N_META, B_QK), lambda b, h, i: (0, h, 0, 0)),
            pl.BlockSpec((None, None, N_META, B_V), lambda b, h, i: (0, h, 0, 0)),
        ],
        out_specs=pl.BlockSpec((tq, B_V), lambda b, h, i: (b * nq + i, h)),
        out_shape=jax.ShapeDtypeStruct((batch * seq, heads * B_V), BF16),
        compiler_params=pltpu.CompilerParams(
            dimension_semantics=("arbitrary", "arbitrary", "arbitrary"), vmem_limit_bytes=VMEM_LIMIT),
        name="mla_attn",
    )(qm, km, vm, km_meta, vm_meta)


def _out_proj_kernel(oa_ref, ob_ref, h0_ref, wa_ref, wb_ref, g_ref, b_ref, wr_hi_ref, wr_lo_ref,
                     h1p_ref, lg_ref):
    mix = jnp.dot(oa_ref[...], wa_ref[...], preferred_element_type=F32)
    mix = mix + jnp.dot(ob_ref[...], wb_ref[...], preferred_element_type=F32)
    h1 = _layer_norm(DN_ALPHA * h0_ref[...].astype(F32) + mix, g_ref[...], b_ref[...])
    hi = h1.astype(BF16)
    hi_f = hi.astype(F32)
    lo = (h1 - hi_f).astype(BF16)
    lg = lax.dot_general(wr_hi_ref[...], hi, NT_DIMS, preferred_element_type=F32)
    lg = lg + lax.dot_general(wr_hi_ref[...], lo, NT_DIMS, preferred_element_type=F32)
    lg = lg + lax.dot_general(wr_lo_ref[...], hi, NT_DIMS, preferred_element_type=F32)
    lg_ref[...] = lg
    h1p_ref[...] = _pack_halves(hi_f)


def _out_proj(oa, ob, h0, wa, wb, g, b, wr_hi, wr_lo, tm):
    n = oa.shape[0]
    row = lambda i: (i, 0)
    return pl.pallas_call(
        _out_proj_kernel,
        grid=(n // tm,),
        in_specs=[
            pl.BlockSpec((tm, A_Q_W), row),
            pl.BlockSpec((tm, B_HEADS * B_V), row),
            pl.BlockSpec((tm, D_MODEL), row),
            _const_spec(wa.shape), _const_spec(wb.shape),
            _const_spec((1, D_MODEL)), _const_spec((1, D_MODEL)),
            _const_spec(wr_hi.shape), _const_spec(wr_lo.shape),
        ],
        out_specs=[
            pl.BlockSpec((tm, HALF), row),
            pl.BlockSpec((N_EXPERTS, tm), lambda i: (0, i)),
        ],
        out_shape=[
            jax.ShapeDtypeStruct((n, HALF), U32),
            jax.ShapeDtypeStruct((N_EXPERTS, n), F32),
        ],
        compiler_params=pltpu.CompilerParams(
            dimension_semantics=("arbitrary",), vmem_limit_bytes=VMEM_LIMIT),
        name="out_proj",
    )(oa, ob, h0, wa, wb, g, b, wr_hi, wr_lo)


def _first_max(cur, idx, sentinel):
    m = jnp.max(cur, axis=0, keepdims=True)
    am = jnp.min(jnp.where(cur == m, idx, sentinel), axis=0, keepdims=True)
    return m, am


def _route_kernel(lg_ref, bias_ref, eidx_ref, wts_ref, rank_ref, cnt_ref, carry_ref):
    @pl.when(pl.program_id(0) == 0)
    def _():
        carry_ref[...] = jnp.zeros_like(carry_ref)

    t = lg_ref.shape[1]
    scores = jax.nn.sigmoid(lg_ref[...])
    sel = scores + bias_ref[...]
    neg_inf = jnp.float32(-jnp.inf)

    r32 = lax.broadcasted_iota(I32, (GROUP_SIZE, t), 0)
    gs = []
    for g in range(N_GROUPS):
        blk = sel[g * GROUP_SIZE:(g + 1) * GROUP_SIZE]
        m1, i1 = _first_max(blk, r32, GROUP_SIZE)
        m2 = jnp.max(jnp.where(r32 == i1, neg_inf, blk), axis=0, keepdims=True)
        gs.append(m1 + m2)
    cur = jnp.concatenate(gs, axis=0)
    r8 = lax.broadcasted_iota(I32, (N_GROUPS, t), 0)
    keep = r8 < 0
    for _ in range(TOPK_GROUPS):
        _, ig = _first_max(cur, r8, N_GROUPS)
        hit = r8 == ig
        keep = keep | hit
        cur = jnp.where(hit, neg_inf, cur)
    cur = jnp.concatenate(
        [jnp.where(keep[g:g + 1], sel[g * GROUP_SIZE:(g + 1) * GROUP_SIZE], NEG_INF) for g in range(N_GROUPS)],
        axis=0)

    row = lax.broadcasted_iota(I32, (N_EXPERTS, t), 0)
    chosen = jnp.zeros((N_EXPERTS, t), F32)
    idxs, ws = [], []
    for _ in range(TOP_K):
        _, ik = _first_max(cur, row, N_EXPERTS)
        hit = row == ik
        ws.append(jnp.sum(jnp.where(hit, scores, 0.0), axis=0, keepdims=True))
        cur = jnp.where(hit, neg_inf, cur)
        chosen = jnp.where(hit, 1.0, chosen)
        idxs.append(ik)
    eidx = jnp.concatenate(idxs, axis=0)
    w = jnp.concatenate(ws, axis=0)
    eidx_ref[...] = eidx
    wts_ref[...] = w / jnp.sum(w, axis=0, keepdims=True) * ROUTED_SCALE

    before = lax.broadcasted_iota(I32, (t, t), 0) < lax.broadcasted_iota(I32, (t, t), 1)
    prefix = jnp.dot(chosen.astype(BF16), jnp.where(before, 1.0, 0.0).astype(BF16), preferred_element_type=F32)
    prefix = prefix + carry_ref[...]
    ranks = [jnp.sum(jnp.where(row == idxs[k], prefix, 0.0), axis=0, keepdims=True) for k in range(TOP_K)]
    rank_ref[...] = jnp.concatenate(ranks, axis=0).astype(I32)
    carry_ref[...] = carry_ref[...] + jnp.sum(chosen, axis=1, keepdims=True)
    cnt_ref[...] = carry_ref[...].astype(I32)


def _route(logits_t, bias, t):
    n = logits_t.shape[1]
    col = lambda i: (0, i)
    return pl.pallas_call(
        _route_kernel,
        grid=(n // t,),
        in_specs=[pl.BlockSpec((N_EXPERTS, t), col), _const_spec((N_EXPERTS, 1))],
        out_specs=[
            pl.BlockSpec((TOP_K, t), col), pl.BlockSpec((TOP_K, t), col), pl.BlockSpec((TOP_K, t), col),
            pl.BlockSpec((N_EXPERTS, 1), lambda i: (0, 0)),
        ],
        out_shape=[
            jax.ShapeDtypeStruct((TOP_K, n), I32),
            jax.ShapeDtypeStruct((TOP_K, n), F32),
            jax.ShapeDtypeStruct((TOP_K, n), I32),
            jax.ShapeDtypeStruct((N_EXPERTS, 1), I32),
        ],
        scratch_shapes=[pltpu.VMEM((N_EXPERTS, 1), F32)],
        compiler_params=pltpu.CompilerParams(
            dimension_semantics=("arbitrary",), vmem_limit_bytes=VMEM_LIMIT),
        name="route",
    )(logits_t, bias)


def _positions_kernel(eidx_ref, rank_ref, start_ref, pos_ref):
    t = eidx_ref.shape[1]
    row = lax.broadcasted_iota(I32, (N_EXPERTS, t), 0)
    start = start_ref[...]
    eidx = eidx_ref[...]
    base = [jnp.sum(jnp.where(row == eidx[k:k + 1], start, 0), axis=0, keepdims=True) for k in range(TOP_K)]
    pos_ref[...] = jnp.concatenate(base, axis=0) + rank_ref[...]


def _positions(eidx, rank, starts, t):
    n = eidx.shape[1]
    col = lambda i: (0, i)
    return pl.pallas_call(
        _positions_kernel,
        grid=(n // t,),
        in_specs=[pl.BlockSpec((TOP_K, t), col), pl.BlockSpec((TOP_K, t), col), _const_spec((N_EXPERTS, 1))],
        out_specs=pl.BlockSpec((None, TOP_K, t), lambda i: (i, 0, 0)),
        out_shape=jax.ShapeDtypeStruct((n // t, TOP_K, t), I32),
        compiler_params=pltpu.CompilerParams(dimension_semantics=("arbitrary",)),
        name="positions",
    )(eidx, rank, starts)


def _zero_fill(row0_ref, len_ref, nu_ref, zbuf, xs_ref, sem, wait):
    def run(copy):
        copy.wait() if wait else copy.start()

    def per_expert(e, carry):
        r0 = row0_ref[e]
        ln = len_ref[e]
        head = jnp.minimum((-r0) & (SUBLANES - 1), ln)
        for j in range(SUBLANES - 1):
            @pl.when(j < head)
            def _(j=j):
                run(pltpu.make_async_copy(zbuf.at[pl.ds(0, 1)], xs_ref.at[pl.ds(r0 + j, 1)], sem))
        r8 = r0 + head
        l8 = ln - head
        bit = MOE_BLOCK // 2
        while bit >= SUBLANES:
            off = l8 - (l8 & (2 * bit - 1))

            @pl.when((l8 & bit) != 0)
            def _(bit=bit, off=off):
                row = pl.multiple_of(r8 + off, SUBLANES)
                run(pltpu.make_async_copy(zbuf.at[pl.ds(0, bit)], xs_ref.at[pl.ds(row, bit)], sem))
            bit //= 2
        return carry

    lax.fori_loop(0, N_EXPERTS, per_expert, 0)

    def per_block(g, carry):
        row = pl.multiple_of(g * MOE_BLOCK, MOE_BLOCK)
        run(pltpu.make_async_copy(zbuf, xs_ref.at[pl.ds(row, MOE_BLOCK)], sem))
        return carry

    lax.fori_loop(nu_ref[0], xs_ref.shape[0] // MOE_BLOCK, per_block, 0)


def _dispatch_kernel(row0_ref, len_ref, nu_ref, pos_ref, h_ref, xs_ref, zbuf, sem, zsem):
    td = h_ref.shape[0]

    @pl.when(pl.program_id(0) == 0)
    def _():
        zbuf[...] = jnp.zeros_like(zbuf)
        _zero_fill(row0_ref, len_ref, nu_ref, zbuf, xs_ref, zsem, wait=False)

    def issue(t, carry):
        for k in range(TOP_K):
            pltpu.make_async_copy(h_ref.at[pl.ds(t, 1)], xs_ref.at[pl.ds(pos_ref[k, t], 1)], sem).start()
        return carry

    lax.fori_loop(0, td, issue, 0)
    for _ in range(TOP_K):
        pltpu.make_async_copy(h_ref, xs_ref.at[pl.ds(0, td)], sem).wait()

    @pl.when(pl.program_id(0) == pl.num_programs(0) - 1)
    def _():
        _zero_fill(row0_ref, len_ref, nu_ref, zbuf, xs_ref, zsem, wait=True)


def _dispatch(pad_row0, pad_len, n_used, pos3, h1p, n_rows, td):
    n = h1p.shape[0]
    tp = pos3.shape[2]
    per = tp // td
    return pl.pallas_call(
        _dispatch_kernel,
        grid_spec=pltpu.PrefetchScalarGridSpec(
            num_scalar_prefetch=3,
            grid=(n // td,),
            in_specs=[
                pl.BlockSpec((None, TOP_K, td), lambda i, *_: (i // per, 0, i % per), memory_space=pltpu.SMEM),
                pl.BlockSpec((td, HALF), lambda i, *_: (i, 0)),
            ],
            out_specs=pl.BlockSpec(memory_space=pl.ANY),
            scratch_shapes=[pltpu.VMEM((MOE_BLOCK, HALF), U32), pltpu.SemaphoreType.DMA, pltpu.SemaphoreType.DMA],
        ),
        out_shape=jax.ShapeDtypeStruct((n_rows, HALF), U32),
        compiler_params=pltpu.CompilerParams(dimension_semantics=("arbitrary",)),
        name="dispatch",
    )(pad_row0, pad_len, n_used, pos3, h1p)


def _moe_kernel(be_ref, nv_ref, nu_ref, x_ref, wg_ref, wu_ref, wd_ref, y_ref, wg_s, wu_s, wd_s):
    g = pl.program_id(0)
    nv = nv_ref[g]
    new_expert = (g == 0) | (be_ref[g] != be_ref[jnp.maximum(g - 1, 0)])

    @pl.when((nv > 0) & new_expert)
    def _():
        wg_s[...] = wg_ref[...].astype(BF16)
        wu_s[...] = wu_ref[...].astype(BF16)
        wd_s[...] = wd_ref[...].astype(BF16)

    @pl.when(nv > 0)
    def _():
        rows = lax.broadcasted_iota(I32, x_ref.shape, 0)
        u = jnp.where(rows < nv, x_ref[...], jnp.uint32(0))
        lo, hi = _unpack_halves(u)
        lo = lo.astype(BF16)
        hi = hi.astype(BF16)
        gate = jnp.dot(lo, wg_s[:HALF], preferred_element_type=F32) + jnp.dot(hi, wg_s[HALF:], preferred_element_type=F32)
        up = jnp.dot(lo, wu_s[:HALF], preferred_element_type=F32) + jnp.dot(hi, wu_s[HALF:], preferred_element_type=F32)
        act = (_silu(gate) * up).astype(BF16)
        y = jnp.dot(act, wd_s[...], preferred_element_type=F32)
        y_ref[...] = _pack_halves(y.astype(BF16).astype(F32))

    @pl.when(nv <= 0)
    def _():
        y_ref[...] = jnp.zeros_like(y_ref)


def _moe(block_e, n_valid, n_used, xs, w_gate, w_up, w_down):
    n_rows = xs.shape[0]
    nblk = n_rows // MOE_BLOCK
    used = lambda g, be, nv, nu: (jnp.minimum(g, nu[0] - 1), 0)
    wsel = lambda g, be, nv, nu: (be[g], 0, 0)
    yout = lambda g, be, nv, nu: (g, 0)
    return pl.pallas_call(
        _moe_kernel,
        grid_spec=pltpu.PrefetchScalarGridSpec(
            num_scalar_prefetch=3,
            grid=(nblk,),
            in_specs=[
                pl.BlockSpec((MOE_BLOCK, HALF), used),
                pl.BlockSpec((None, D_MODEL, D_EXPERT), wsel),
                pl.BlockSpec((None, D_MODEL, D_EXPERT), wsel),
                pl.BlockSpec((None, D_EXPERT, D_MODEL), wsel),
            ],
            out_specs=pl.BlockSpec((MOE_BLOCK, HALF), yout),
            scratch_shapes=[
                pltpu.VMEM((D_MODEL, D_EXPERT), BF16),
                pltpu.VMEM((D_MODEL, D_EXPERT), BF16),
                pltpu.VMEM((D_EXPERT, D_MODEL), BF16),
            ],
        ),
        out_shape=jax.ShapeDtypeStruct((n_rows, HALF), U32),
        compiler_params=pltpu.CompilerParams(
            dimension_semantics=("arbitrary",), vmem_limit_bytes=VMEM_LIMIT),
        name="moe",
    )(block_e, n_valid, n_used, xs, w_gate, w_up, w_down)


def _combine_kernel(pos_ref, h_ref, wts_ref, ys_ref, wsg_ref, wsu_ref, wsd_ref, g_ref, b_ref,
                    o_ref, gbuf, sem):
    tc = h_ref.shape[0]

    def issue(t, carry):
        for k in range(TOP_K):
            pltpu.make_async_copy(ys_ref.at[pl.ds(pos_ref[k, t], 1)], gbuf.at[k, pl.ds(t, 1)], sem).start()
        return carry

    lax.fori_loop(0, tc, issue, 0)

    lo, hi = _unpack_halves(h_ref[...])
    h = jnp.concatenate([lo, hi], axis=1)
    hb = h.astype(BF16)
    act = _silu(jnp.dot(hb, wsg_ref[...], preferred_element_type=F32))
    act = (act * jnp.dot(hb, wsu_ref[...], preferred_element_type=F32)).astype(BF16)
    shared = jnp.dot(act, wsd_ref[...], preferred_element_type=F32)

    for k in range(TOP_K):
        pltpu.make_async_copy(ys_ref.at[pl.ds(0, tc)], gbuf.at[k], sem).wait()

    acc_lo = shared[:, :HALF]
    acc_hi = shared[:, HALF:]
    wts = wts_ref[...]
    for k in range(TOP_K):
        y_lo, y_hi = _unpack_halves(gbuf[k])
        wk = wts[:, k:k + 1]
        acc_lo = acc_lo + wk * y_lo
        acc_hi = acc_hi + wk * y_hi
    z = DN_ALPHA * h + jnp.concatenate([acc_lo, acc_hi], axis=1)
    o_ref[...] = _layer_norm(z, g_ref[...], b_ref[...])


def _combine(pos3, h1p, wts_t, ys, wsg, wsu, wsd, g, b, tc):
    n = h1p.shape[0]
    tp = pos3.shape[2]
    per = tp // tc
    row = lambda i: (i, 0)
    return pl.pallas_call(
        _combine_kernel,
        grid=(n // tc,),
        in_specs=[
            pl.BlockSpec((None, TOP_K, tc), lambda i: (i // per, 0, i % per), memory_space=pltpu.SMEM),
            pl.BlockSpec((tc, HALF), row),
            pl.BlockSpec((tc, TOP_K), row),
            pl.BlockSpec(memory_space=pl.ANY),
            _const_spec(wsg.shape), _const_spec(wsu.shape), _const_spec(wsd.shape),
            _const_spec((1, D_MODEL)), _const_spec((1, D_MODEL)),
        ],
        out_specs=pl.BlockSpec((tc, D_MODEL), row),
        out_shape=jax.ShapeDtypeStruct((n, D_MODEL), F32),
        scratch_shapes=[pltpu.VMEM((TOP_K, tc, HALF), U32), pltpu.SemaphoreType.DMA],
        compiler_params=pltpu.CompilerParams(
            dimension_semantics=("arbitrary",), vmem_limit_bytes=VMEM_LIMIT),
        name="combine",
    )(pos3, h1p, wts_t, ys, wsg, wsu, wsd, g, b)


def _rotate_half_cols(w):
    half = w.shape[-1] // 2
    return jnp.concatenate([-w[..., half:], w[..., :half]], axis=-1)


def _rope_tables(pos):
    inv_freq = ROPE_THETA ** (-jnp.arange(0, B_ROPE, 2, dtype=F32) / B_ROPE)
    ang = pos[:, None] * inv_freq[None, :]
    cos = jnp.tile(jnp.cos(ang), (1, 2 * B_HEADS))
    sin = jnp.tile(jnp.sin(ang), (1, 2 * B_HEADS))
    return cos, sin


def kernel(x, meta_tokens, ln_emb_g, ln_emb_b, w_in, q_norm_g, w_q_up, kv_norm_g, w_kv_up, sink_logits, w_out,
           ln_mix_g, ln_mix_b, w_router, router_bias, w_exp_gate, w_exp_up, w_exp_down, w_sh_gate, w_sh_up,
           w_sh_down, ln_ffn_g, ln_ffn_b):
    batch, seq, d = x.shape
    assert d == D_MODEL and seq % 512 == 0 and w_in.shape[0] == DEPTH
    n = batch * seq
    row2 = lambda v: v.reshape(1, -1).astype(F32)

    w_in0 = w_in[0]
    kpe_cols = w_in0[:, -B_ROPE:]
    w_in_b = jnp.concatenate([w_in0, _rotate_half_cols(kpe_cols)], axis=1).astype(BF16)
    wq = w_q_up[0]
    wq_pe = wq[:, :, B_NOPE:]
    wq_b = jnp.concatenate([
        wq[:, :, :B_NOPE].reshape(B_Q_RANK, -1),
        wq_pe.reshape(B_Q_RANK, -1),
        _rotate_half_cols(wq_pe).reshape(B_Q_RANK, -1)], axis=1).astype(BF16)
    wkv = w_kv_up[0]
    wkv_b = jnp.concatenate([
        wkv[:, :, :B_NOPE].reshape(B_KV_RANK, -1), wkv[:, :, B_NOPE:].reshape(B_KV_RANK, -1)], axis=1).astype(BF16)
    w_out_b = w_out[0].astype(BF16)
    wr_t = w_router[0].astype(F32).T
    wr_hi = wr_t.astype(BF16)
    wr_lo = (wr_t - wr_hi.astype(F32)).astype(BF16)

    cos_m, sin_m = _rope_tables(jnp.arange(N_META, dtype=F32))
    cos_x, sin_x = _rope_tables(jnp.arange(N_META, N_META + seq, dtype=F32))
    proj_args = (row2(ln_emb_g), row2(ln_emb_b), w_in_b, row2(q_norm_g[0]), wq_b, row2(kv_norm_g[0]), wkv_b)
    _, _, ka_m, va_m, _, km_m, vm_m = _embed_proj(
        meta_tokens.astype(F32), 1, N_META, N_META, *proj_args, cos_m, sin_m)
    h0, qa, ka, va, qm, km, vm = _embed_proj(x.reshape(n, d), batch, seq, 256, *proj_args, cos_x, sin_x)

    pad = jnp.zeros((BLK - N_META, A_KV_W), BF16)
    oa = _win_attn(sink_logits[0].astype(F32), qa, ka, va,
                   jnp.concatenate([pad, ka_m], axis=0), jnp.concatenate([pad, va_m], axis=0), batch, seq)
    ob = _mla_attn(qm, km, vm, km_m, vm_m, 512)

    h1p, logits_t = _out_proj(oa, ob, h0, w_out_b[:A_Q_W], w_out_b[A_Q_W:], row2(ln_mix_g[0]), row2(ln_mix_b[0]),
                              wr_hi, wr_lo, 256)

    eidx, wts, rank, counts = _route(logits_t, router_bias[0].astype(F32).reshape(N_EXPERTS, 1), 512)

    counts = counts[:, 0]
    nk = n * TOP_K
    padded = (counts + MOE_BLOCK - 1) // MOE_BLOCK * MOE_BLOCK
    pad_ends = jnp.cumsum(padded)
    pad_starts = pad_ends - padded
    n_rows = (nk + MOE_BLOCK - 1) // MOE_BLOCK * MOE_BLOCK + N_EXPERTS * MOE_BLOCK
    n_blocks = n_rows // MOE_BLOCK
    blk_row0 = jnp.arange(n_blocks, dtype=I32) * MOE_BLOCK
    block_e = jnp.minimum(jnp.searchsorted(pad_ends, blk_row0, side='right'), N_EXPERTS - 1).astype(I32)
    n_valid = jnp.clip(pad_starts[block_e] + counts[block_e] - blk_row0, 0, MOE_BLOCK).astype(I32)
    n_used = (pad_ends[-1:] // MOE_BLOCK).astype(I32)

    pos3 = _positions(eidx, rank, pad_starts.astype(I32).reshape(N_EXPERTS, 1), 512)
    xs = _dispatch((pad_starts + counts).astype(I32), (padded - counts).astype(I32), n_used, pos3, h1p, n_rows, 256)
    ys = _moe(block_e, n_valid, n_used, xs, w_exp_gate[0], w_exp_up[0], w_exp_down[0])
    out = _combine(pos3, h1p, wts.T, ys, w_sh_gate[0].astype(BF16), w_sh_up[0].astype(BF16),
                   w_sh_down[0].astype(BF16), row2(ln_ffn_g[0]), row2(ln_ffn_b[0]), 128)
    return out.reshape(batch, seq, d)
```

```python
import functools

import jax
import jax.numpy as jnp
from jax import lax
from jax.experimental import pallas as pl
from jax.experimental.pallas import tpu as pltpu

F32 = jnp.float32
BF16 = jnp.bfloat16
I32 = jnp.int32

D_MODEL = 2048
N_META = 16
BLK = 128
A_HEADS = 16
A_KV_HEADS = 4
A_HEAD_DIM = 64
A_WINDOW = 128
B_HEADS = 8
B_Q_RANK = 512
B_KV_RANK = 256
B_NOPE = 128
B_ROPE = 64
B_V = 128
ROPE_THETA = 10000.0
A_Q_W = A_HEADS * A_HEAD_DIM
A_KV_W = A_KV_HEADS * A_HEAD_DIM
N_EXPERTS = 256
N_GROUPS = 8
GROUP_SIZE = N_EXPERTS // N_GROUPS
TOPK_GROUPS = 4
TOP_K = 8
D_EXPERT = 512
D_SHARED = 512
ROUTED_SCALE = 2.5
MOE_BLOCK = 256
DEPTH = 1
DN_ALPHA = (2 * DEPTH) ** 0.25
LN_EPS = 1e-5
RMS_EPS = 1e-6
NEG_INF = -1e30

A_SCALE = A_HEAD_DIM ** -0.5
B_SCALE = (B_NOPE + B_ROPE) ** -0.5
B_QK = B_NOPE + B_ROPE

SUBLANES = 8
LANES = 128
ROW_TILE = (D_MODEL // LANES, LANES)
assert ROW_TILE[0] == 2 * SUBLANES

VMEM_LIMIT = 56 * 1024 * 1024

NT_DIMS = (((1,), (1,)), ((), ()))


def _const_spec(shape):
    zeros = (0,) * len(shape)
    return pl.BlockSpec(shape, lambda *_: zeros, pipeline_mode=pl.Buffered(1))


def _layer_norm(z, g, b):
    mu = jnp.mean(z, axis=-1, keepdims=True)
    zc = z - mu
    var = jnp.mean(zc * zc, axis=-1, keepdims=True)
    return zc * lax.rsqrt(var + LN_EPS) * g + b


def _rms_norm(z, g):
    return z * lax.rsqrt(jnp.mean(z * z, axis=-1, keepdims=True) + RMS_EPS) * g


def _silu(z):
    return z * jax.nn.sigmoid(z)


def _embed_proj_kernel(x_ref, g_ref, b_ref, win_ref, gq_ref, wq_ref, gkv_ref, wkv_ref, cos_ref, sin_ref, rep_ref,
                       h0_ref, qa_ref, ka_ref, va_ref, qm_ref, km_ref, vm_ref):
    h = _layer_norm(x_ref[...], g_ref[...], b_ref[...])
    hb = h.astype(BF16)
    h0_ref[...] = hb
    p = jnp.dot(hb, win_ref[...], preferred_element_type=F32)
    o = 0
    qa_ref[...] = (p[:, o:o + A_Q_W] * A_SCALE).astype(BF16)
    o += A_Q_W
    ka_ref[...] = jnp.dot(p[:, o:o + A_KV_W].astype(BF16), rep_ref[...], preferred_element_type=F32).astype(BF16)
    o += A_KV_W
    va_ref[...] = jnp.dot(p[:, o:o + A_KV_W].astype(BF16), rep_ref[...], preferred_element_type=F32).astype(BF16)
    o += A_KV_W
    cq = p[:, o:o + B_Q_RANK]
    o += B_Q_RANK
    ckv = p[:, o:o + B_KV_RANK]
    o += B_KV_RANK
    kpe = p[:, o:o + B_ROPE]
    o += B_ROPE
    kpe_rot = p[:, o:o + B_ROPE]

    cos = cos_ref[...]
    sin = sin_ref[...]
    qall = jnp.dot(_rms_norm(cq, gq_ref[...]).astype(BF16), wq_ref[...], preferred_element_type=F32)
    n_nope = B_HEADS * B_NOPE
    n_pe = B_HEADS * B_ROPE
    q_pe = qall[:, n_nope:n_nope + n_pe] * cos + qall[:, n_nope + n_pe:] * sin
    kv = jnp.dot(_rms_norm(ckv, gkv_ref[...]).astype(BF16), wkv_ref[...], preferred_element_type=F32)
    k_pe = (kpe * cos[:, :B_ROPE] + kpe_rot * sin[:, :B_ROPE]).astype(BF16)
    for hd in range(B_HEADS):
        qn = (qall[:, hd * B_NOPE:(hd + 1) * B_NOPE] * B_SCALE).astype(BF16)
        qp = (q_pe[:, hd * B_ROPE:(hd + 1) * B_ROPE] * B_SCALE).astype(BF16)
        qm_ref[hd] = jnp.concatenate([qn, qp], axis=1)
        kn = kv[:, hd * B_NOPE:(hd + 1) * B_NOPE].astype(BF16)
        km_ref[hd] = jnp.concatenate([kn, k_pe], axis=1)
        vm_ref[hd] = kv[:, n_nope + hd * B_V:n_nope + (hd + 1) * B_V].astype(BF16)


def _embed_proj(x2, batch, seq, tm, ln_g, ln_b, w_in, gq, wq, gkv, wkv, rep, cos, sin):
    n = batch * seq
    nt = seq // tm
    row = lambda b, i: (b * nt + i, 0)
    head_major = lambda b, i: (b, 0, i, 0)
    in_w = w_in.shape[1]
    return pl.pallas_call(
        _embed_proj_kernel,
        grid=(batch, nt),
        in_specs=[
            pl.BlockSpec((tm, D_MODEL), row),
            _const_spec((1, D_MODEL)), _const_spec((1, D_MODEL)),
            _const_spec((D_MODEL, in_w)),
            _const_spec((1, B_Q_RANK)), _const_spec(wq.shape),
            _const_spec((1, B_KV_RANK)), _const_spec(wkv.shape),
            pl.BlockSpec((tm, B_HEADS * B_ROPE), lambda b, i: (i, 0)),
            pl.BlockSpec((tm, B_HEADS * B_ROPE), lambda b, i: (i, 0)),
            _const_spec(rep.shape),
        ],
        out_specs=[
            pl.BlockSpec((tm, D_MODEL), row),
            pl.BlockSpec((tm, A_Q_W), row),
            pl.BlockSpec((tm, A_Q_W), row),
            pl.BlockSpec((tm, A_Q_W), row),
            pl.BlockSpec((None, B_HEADS, tm, B_QK), head_major),
            pl.BlockSpec((None, B_HEADS, tm, B_QK), head_major),
            pl.BlockSpec((None, B_HEADS, tm, B_V), head_major),
        ],
        out_shape=[
            jax.ShapeDtypeStruct((n, D_MODEL), BF16),
            jax.ShapeDtypeStruct((n, A_Q_W), BF16),
            jax.ShapeDtypeStruct((n, A_Q_W), BF16),
            jax.ShapeDtypeStruct((n, A_Q_W), BF16),
            jax.ShapeDtypeStruct((batch, B_HEADS, seq, B_QK), BF16),
            jax.ShapeDtypeStruct((batch, B_HEADS, seq, B_QK), BF16),
            jax.ShapeDtypeStruct((batch, B_HEADS, seq, B_V), BF16),
        ],
        compiler_params=pltpu.CompilerParams(
            dimension_semantics=("arbitrary", "arbitrary"), vmem_limit_bytes=VMEM_LIMIT),
        name="embed_proj",
    )(x2, ln_g, ln_b, w_in, gq, wq, gkv, wkv, cos, sin, rep)


def _win_attn_kernel(sink_ref, q_ref, kp_ref, kc_ref, kn_ref, vp_ref, vc_ref, vn_ref, kmeta_ref, vmeta_ref,
                     bias_ref, o_ref):
    first = pl.program_id(1) == 0
    grp = A_HEADS // A_KV_HEADS
    gw = grp * A_HEAD_DIM
    k3 = jnp.concatenate([jnp.where(first, kmeta_ref[...], kp_ref[...]), kc_ref[...], kn_ref[...]], axis=0)
    v3 = jnp.concatenate([jnp.where(first, vmeta_ref[...], vp_ref[...]), vc_ref[...], vn_ref[...]], axis=0)
    lane_head = lax.broadcasted_iota(I32, (BLK, gw), 1) // A_HEAD_DIM
    row_head = lax.broadcasted_iota(I32, (grp * BLK, 1), 0) // BLK
    for kvh in range(A_KV_HEADS):
        cols = slice(kvh * gw, (kvh + 1) * gw)
        qg = q_ref[:, cols]
        q4 = jnp.concatenate([jnp.where(lane_head == r, qg, jnp.zeros_like(qg)) for r in range(grp)], axis=0)
        sink = jnp.zeros((grp * BLK, 1), F32)
        for r in range(grp):
            sink = jnp.where(row_head == r, sink_ref[kvh * grp + r], sink)
        s = lax.dot_general(q4, k3[:, cols], NT_DIMS, preferred_element_type=F32) + bias_ref[kvh]
        m = jnp.maximum(jnp.max(s, axis=-1, keepdims=True), sink)
        p = jnp.exp(s - m)
        denom = jnp.sum(p, axis=-1, keepdims=True) + jnp.exp(sink - m)
        ow = jnp.dot(p.astype(BF16), v3[:, cols], preferred_element_type=F32) / denom
        og = jnp.zeros((BLK, gw), F32)
        for r in range(grp):
            og = jnp.where(lane_head == r, ow[r * BLK:(r + 1) * BLK], og)
        o_ref[:, cols] = og.astype(BF16)


def _win_bias():
    qi = jnp.arange(BLK)[:, None]
    ki = jnp.arange(3 * BLK)[None, :]
    dist = jnp.abs(BLK + qi - ki)
    in_window = dist <= A_WINDOW
    valid = jnp.stack([in_window, in_window & (ki >= BLK - N_META), in_window & (ki < 2 * BLK)])
    slopes = jnp.exp2(-8.0 * jnp.arange(1, A_HEADS + 1, dtype=F32) / A_HEADS)
    bias = jnp.where(valid[:, None], -slopes[None, :, None, None] * dist.astype(F32)[None, None], NEG_INF)
    return bias.reshape(3, A_KV_HEADS, (A_HEADS // A_KV_HEADS) * BLK, 3 * BLK)


def _win_attn(sink, qa, ka, va, kmeta, vmeta, batch, seq):
    nb = seq // BLK
    n = batch * seq
    bias = _win_bias()
    cur = lambda b, i, *_: (b * nb + i, 0)
    prev = lambda b, i, *_: (b * nb + jnp.maximum(i - 1, 0), 0)
    nxt = lambda b, i, *_: (b * nb + jnp.minimum(i + 1, nb - 1), 0)
    kv_spec = lambda f: pl.BlockSpec((BLK, A_Q_W), f)
    const = lambda b, i, *_: (0, 0)
    variant = lambda b, i, *_: (jnp.where(i == 0, 1, jnp.where(i == nb - 1, 2, 0)), 0, 0, 0)
    return pl.pallas_call(
        _win_attn_kernel,
        grid_spec=pltpu.PrefetchScalarGridSpec(
            num_scalar_prefetch=1,
            grid=(batch, nb),
            in_specs=[
                pl.BlockSpec((BLK, A_Q_W), cur),
                kv_spec(prev), kv_spec(cur), kv_spec(nxt),
                kv_spec(prev), kv_spec(cur), kv_spec(nxt),
                pl.BlockSpec((BLK, A_Q_W), const), pl.BlockSpec((BLK, A_Q_W), const),
                pl.BlockSpec((None,) + bias.shape[1:], variant),
            ],
            out_specs=pl.BlockSpec((BLK, A_Q_W), cur),
        ),
        out_shape=jax.ShapeDtypeStruct((n, A_Q_W), BF16),
        compiler_params=pltpu.CompilerParams(
            dimension_semantics=("arbitrary", "arbitrary"), vmem_limit_bytes=VMEM_LIMIT),
        name="win_attn",
    )(sink, qa, ka, ka, ka, va, va, va, kmeta, vmeta, bias)


MLA_KEY_CHUNKS = 2


def _mla_attn_kernel(q_ref, k_ref, v_ref, km_ref, vm_ref, o_ref):
    q = q_ref[...]
    sm = lax.dot_general(q, km_ref[...], NT_DIMS, preferred_element_type=F32)
    m = jnp.max(sm, axis=-1, keepdims=True)
    pm = jnp.exp(sm - m)
    denom = jnp.sum(pm, axis=-1, keepdims=True)
    acc = jnp.dot(pm.astype(BF16), vm_ref[...], preferred_element_type=F32)
    chunk = k_ref.shape[0] // MLA_KEY_CHUNKS
    for c in range(MLA_KEY_CHUNKS):
        keys = slice(c * chunk, (c + 1) * chunk)
        s = lax.dot_general(q, k_ref[keys, :], NT_DIMS, preferred_element_type=F32)
        m_new = jnp.maximum(m, jnp.max(s, axis=-1, keepdims=True))
        rescale = jnp.exp(m - m_new)
        p = jnp.exp(s - m_new)
        denom = rescale * denom + jnp.sum(p, axis=-1, keepdims=True)
        acc = rescale * acc + jnp.dot(p.astype(BF16), v_ref[keys, :], preferred_element_type=F32)
        m = m_new
    o_ref[...] = (acc / denom).astype(BF16)


def _mla_attn(qm, km, vm, km_meta, vm_meta, tq):
    batch, heads, seq, _ = qm.shape
    nq = seq // tq
    return pl.pallas_call(
        _mla_attn_kernel,
        grid=(batch, heads, nq),
        in_specs=[
            pl.BlockSpec((None, None, tq, B_QK), lambda b, h, i: (b, h, i, 0)),
            pl.BlockSpec((None, None, seq, B_QK), lambda b, h, i: (b, h, 0, 0)),
            pl.BlockSpec((None, None, seq, B_V), lambda b, h, i: (b, h, 0, 0)),
            pl.BlockSpec((None, None, N_META, B_QK), lambda b, h, i: (0, h, 0, 0)),
            pl.BlockSpec((None, None, N_META, B_V), lambda b, h, i: (0, h, 0, 0)),
        ],
        out_specs=pl.BlockSpec((tq, B_V), lambda b, h, i: (b * nq + i, h)),
        out_shape=jax.ShapeDtypeStruct((batch * seq, heads * B_V), BF16),
        compiler_params=pltpu.CompilerParams(
            dimension_semantics=("arbitrary", "arbitrary", "arbitrary"), vmem_limit_bytes=VMEM_LIMIT),
        name="mla_attn",
    )(qm, km, vm, km_meta, vm_meta)


def _out_proj_kernel(oa_ref, ob_ref, h0_ref, wa_ref, wb_ref, g_ref, b_ref, wr_hi_ref, wr_lo_ref,
                     h1_ref, lg_ref):
    mix = jnp.dot(oa_ref[...], wa_ref[...], preferred_element_type=F32)
    mix = mix + jnp.dot(ob_ref[...], wb_ref[...], preferred_element_type=F32)
    h1 = _layer_norm(DN_ALPHA * h0_ref[...].astype(F32) + mix, g_ref[...], b_ref[...])
    hi = h1.astype(BF16)
    hi_f = hi.astype(F32)
    lo = (h1 - hi_f).astype(BF16)
    lg = lax.dot_general(wr_hi_ref[...], hi, NT_DIMS, preferred_element_type=F32)
    lg = lg + lax.dot_general(wr_hi_ref[...], lo, NT_DIMS, preferred_element_type=F32)
    lg = lg + lax.dot_general(wr_lo_ref[...], hi, NT_DIMS, preferred_element_type=F32)
    lg_ref[...] = lg
    h1_ref[...] = hi.reshape(h1_ref.shape)


def _out_proj(oa, ob, h0, wa, wb, g, b, wr_hi, wr_lo, tm):
    n = oa.shape[0]
    row = lambda i: (i, 0)
    return pl.pallas_call(
        _out_proj_kernel,
        grid=(n // tm,),
        in_specs=[
            pl.BlockSpec((tm, A_Q_W), row),
            pl.BlockSpec((tm, B_HEADS * B_V), row),
            pl.BlockSpec((tm, D_MODEL), row),
            _const_spec(wa.shape), _const_spec(wb.shape),
            _const_spec((1, D_MODEL)), _const_spec((1, D_MODEL)),
            _const_spec(wr_hi.shape), _const_spec(wr_lo.shape),
        ],
        out_specs=[
            pl.BlockSpec((tm,) + ROW_TILE, lambda i: (i, 0, 0)),
            pl.BlockSpec((N_EXPERTS, tm), lambda i: (0, i)),
        ],
        out_shape=[
            jax.ShapeDtypeStruct((n,) + ROW_TILE, BF16),
            jax.ShapeDtypeStruct((N_EXPERTS, n), F32),
        ],
        compiler_params=pltpu.CompilerParams(
            dimension_semantics=("arbitrary",), vmem_limit_bytes=VMEM_LIMIT),
        name="out_proj",
    )(oa, ob, h0, wa, wb, g, b, wr_hi, wr_lo)


def _first_max(cur, idx, sentinel):
    m = jnp.max(cur, axis=0, keepdims=True)
    am = jnp.min(jnp.where(cur == m, idx, sentinel), axis=0, keepdims=True)
    return m, am


def _route_kernel(lg_ref, bias_ref, eidx_ref, wts_ref, rank_ref, cnt_ref, carry_ref):
    @pl.when(pl.program_id(0) == 0)
    def _():
        carry_ref[...] = jnp.zeros_like(carry_ref)

    t = lg_ref.shape[1]
    scores = jax.nn.sigmoid(lg_ref[...])
    sel = scores + bias_ref[...]
    neg_inf = jnp.float32(-jnp.inf)

    r32 = lax.broadcasted_iota(I32, (GROUP_SIZE, t), 0)
    gs = []
    for g in range(N_GROUPS):
        blk = sel[g * GROUP_SIZE:(g + 1) * GROUP_SIZE]
        m1, i1 = _first_max(blk, r32, GROUP_SIZE)
        m2 = jnp.max(jnp.where(r32 == i1, neg_inf, blk), axis=0, keepdims=True)
        gs.append(m1 + m2)
    cur = jnp.concatenate(gs, axis=0)
    r8 = lax.broadcasted_iota(I32, (N_GROUPS, t), 0)
    keep = r8 < 0
    for _ in range(TOPK_GROUPS):
        _, ig = _first_max(cur, r8, N_GROUPS)
        hit = r8 == ig
        keep = keep | hit
        cur = jnp.where(hit, neg_inf, cur)
    cur = jnp.concatenate(
        [jnp.where(keep[g:g + 1], sel[g * GROUP_SIZE:(g + 1) * GROUP_SIZE], NEG_INF) for g in range(N_GROUPS)],
        axis=0)

    row = lax.broadcasted_iota(I32, (N_EXPERTS, t), 0)
    chosen = jnp.zeros((N_EXPERTS, t), F32)
    idxs, ws = [], []
    for _ in range(TOP_K):
        _, ik = _first_max(cur, row, N_EXPERTS)
        hit = row == ik
        ws.append(jnp.sum(jnp.where(hit, scores, 0.0), axis=0, keepdims=True))
        cur = jnp.where(hit, neg_inf, cur)
        chosen = jnp.where(hit, 1.0, chosen)
        idxs.append(ik)
    eidx = jnp.concatenate(idxs, axis=0)
    w = jnp.concatenate(ws, axis=0)
    eidx_ref[...] = eidx
    wts_ref[...] = w / jnp.sum(w, axis=0, keepdims=True) * ROUTED_SCALE

    before = lax.broadcasted_iota(I32, (t, t), 0) < lax.broadcasted_iota(I32, (t, t), 1)
    prefix = jnp.dot(chosen.astype(BF16), jnp.where(before, 1.0, 0.0).astype(BF16), preferred_element_type=F32)
    prefix = prefix + carry_ref[...]
    ranks = [jnp.sum(jnp.where(row == idxs[k], prefix, 0.0), axis=0, keepdims=True) for k in range(TOP_K)]
    rank_ref[...] = jnp.concatenate(ranks, axis=0).astype(I32)
    carry_ref[...] = carry_ref[...] + jnp.sum(chosen, axis=1, keepdims=True)
    cnt_ref[...] = carry_ref[...].astype(I32)


def _route(logits_t, bias, t):
    n = logits_t.shape[1]
    col = lambda i: (0, i)
    return pl.pallas_call(
        _route_kernel,
        grid=(n // t,),
        in_specs=[pl.BlockSpec((N_EXPERTS, t), col), _const_spec((N_EXPERTS, 1))],
        out_specs=[
            pl.BlockSpec((TOP_K, t), col), pl.BlockSpec((TOP_K, t), col), pl.BlockSpec((TOP_K, t), col),
            pl.BlockSpec((N_EXPERTS, 1), lambda i: (0, 0)),
        ],
        out_shape=[
            jax.ShapeDtypeStruct((TOP_K, n), I32),
            jax.ShapeDtypeStruct((TOP_K, n), F32),
            jax.ShapeDtypeStruct((TOP_K, n), I32),
            jax.ShapeDtypeStruct((N_EXPERTS, 1), I32),
        ],
        scratch_shapes=[pltpu.VMEM((N_EXPERTS, 1), F32)],
        compiler_params=pltpu.CompilerParams(
            dimension_semantics=("arbitrary",), vmem_limit_bytes=VMEM_LIMIT),
        name="route",
    )(logits_t, bias)


def _positions_kernel(eidx_ref, rank_ref, start_ref, pos_ref):
    t = eidx_ref.shape[1]
    row = lax.broadcasted_iota(I32, (N_EXPERTS, t), 0)
    start = start_ref[...]
    eidx = eidx_ref[...]
    base = [jnp.sum(jnp.where(row == eidx[k:k + 1], start, 0), axis=0, keepdims=True) for k in range(TOP_K)]
    pos_ref[...] = jnp.concatenate(base, axis=0) + rank_ref[...]


def _positions(eidx, rank, starts, t):
    n = eidx.shape[1]
    col = lambda i: (0, i)
    return pl.pallas_call(
        _positions_kernel,
        grid=(n // t,),
        in_specs=[pl.BlockSpec((TOP_K, t), col), pl.BlockSpec((TOP_K, t), col), _const_spec((N_EXPERTS, 1))],
        out_specs=pl.BlockSpec((None, TOP_K, t), lambda i: (i, 0, 0)),
        out_shape=jax.ShapeDtypeStruct((n // t, TOP_K, t), I32),
        compiler_params=pltpu.CompilerParams(dimension_semantics=("arbitrary",)),
        name="positions",
    )(eidx, rank, starts)


def _zero_fill(row0_ref, len_ref, nu_ref, zbuf, xs_ref, sem, wait):
    def run(copy):
        copy.wait() if wait else copy.start()

    def per_expert(e, carry):
        r0 = row0_ref[e]
        ln = len_ref[e]
        bit = MOE_BLOCK // 2
        while bit >= 1:
            off = ln - (ln & (2 * bit - 1))

            @pl.when((ln & bit) != 0)
            def _(bit=bit, off=off):
                run(pltpu.make_async_copy(zbuf.at[pl.ds(0, bit)], xs_ref.at[pl.ds(r0 + off, bit)], sem))
            bit //= 2
        return carry

    lax.fori_loop(0, N_EXPERTS, per_expert, 0)

    def per_block(g, carry):
        run(pltpu.make_async_copy(zbuf, xs_ref.at[pl.ds(g * MOE_BLOCK, MOE_BLOCK)], sem))
        return carry

    lax.fori_loop(nu_ref[0], xs_ref.shape[0] // MOE_BLOCK, per_block, 0)


def _dispatch_kernel(row0_ref, len_ref, nu_ref, pos_ref, h_ref, xs_ref, zbuf, sem, zsem):
    td = h_ref.shape[0]

    @pl.when(pl.program_id(0) == 0)
    def _():
        zbuf[...] = jnp.zeros_like(zbuf)
        _zero_fill(row0_ref, len_ref, nu_ref, zbuf, xs_ref, zsem, wait=False)

    def issue(t, carry):
        for k in range(TOP_K):
            pltpu.make_async_copy(h_ref.at[t], xs_ref.at[pos_ref[k, t]], sem).start(priority=k % 2)
        return carry

    lax.fori_loop(0, td, issue, 0, unroll=4)
    for _ in range(TOP_K):
        pltpu.make_async_copy(h_ref, xs_ref.at[pl.ds(0, td)], sem).wait()

    @pl.when(pl.program_id(0) == pl.num_programs(0) - 1)
    def _():
        _zero_fill(row0_ref, len_ref, nu_ref, zbuf, xs_ref, zsem, wait=True)


def _dispatch(pad_row0, pad_len, n_used, pos3, h1, n_rows, td):
    n = h1.shape[0]
    tp = pos3.shape[2]
    per = tp // td
    return pl.pallas_call(
        _dispatch_kernel,
        grid_spec=pltpu.PrefetchScalarGridSpec(
            num_scalar_prefetch=3,
            grid=(n // td,),
            in_specs=[
                pl.BlockSpec((None, TOP_K, td), lambda i, *_: (i // per, 0, i % per), memory_space=pltpu.SMEM),
                pl.BlockSpec((td,) + ROW_TILE, lambda i, *_: (i, 0, 0)),
            ],
            out_specs=pl.BlockSpec(memory_space=pl.ANY),
            scratch_shapes=[pltpu.VMEM((MOE_BLOCK,) + ROW_TILE, BF16), pltpu.SemaphoreType.DMA,
                            pltpu.SemaphoreType.DMA],
        ),
        out_shape=jax.ShapeDtypeStruct((n_rows,) + ROW_TILE, BF16),
        compiler_params=pltpu.CompilerParams(dimension_semantics=("arbitrary",)),
        name="dispatch",
    )(pad_row0, pad_len, n_used, pos3, h1)


CAST_CHUNK_ELEMS = 32 * 1024


def _cast_rows(src, dst):
    chunk = CAST_CHUNK_ELEMS // src.shape[1]

    def body(i, carry):
        r = pl.multiple_of(i * chunk, chunk)
        dst[pl.ds(r, chunk), :] = src[pl.ds(r, chunk), :].astype(dst.dtype)
        return carry

    lax.fori_loop(0, src.shape[0] // chunk, body, 0)


def _moe_kernel(be_ref, nv_ref, nu_ref, nxt_ref, x_ref, wg_hbm, wu_hbm, wd_hbm, y_ref,
                wg_land, wu_land, wd_land, wg_s, wu_s, wd_s, sems):
    g = pl.program_id(0)
    nv = nv_ref[g]
    expert = be_ref[g]
    new_expert = (g == 0) | (expert != be_ref[jnp.maximum(g - 1, 0)])

    def weight_copies(e):
        return (pltpu.make_async_copy(wg_hbm.at[e], wg_land, sems.at[0]),
                pltpu.make_async_copy(wu_hbm.at[e], wu_land, sems.at[1]),
                pltpu.make_async_copy(wd_hbm.at[e], wd_land, sems.at[2]))

    @pl.when(g == 0)
    def _():
        for c in weight_copies(expert):
            c.start()

    @pl.when((nv > 0) & new_expert)
    def _():
        for c in weight_copies(expert):
            c.wait()
        _cast_rows(wg_land, wg_s)
        _cast_rows(wu_land, wu_s)
        _cast_rows(wd_land, wd_s)
        nxt = nxt_ref[g]

        @pl.when(nxt >= 0)
        def _():
            for c in weight_copies(nxt):
                c.start()

    @pl.when(nv > 0)
    def _():
        x = x_ref[...].reshape(MOE_BLOCK, D_MODEL)
        gate = jnp.dot(x, wg_s[...], preferred_element_type=F32)
        up = jnp.dot(x, wu_s[...], preferred_element_type=F32)
        act = (_silu(gate) * up).astype(BF16)
        y = jnp.dot(act, wd_s[...], preferred_element_type=F32)
        y_ref[...] = y.astype(BF16).reshape(y_ref.shape)

    @pl.when(nv <= 0)
    def _():
        y_ref[...] = jnp.zeros_like(y_ref)


def _moe(block_e, n_valid, n_used, next_e, xs, w_gate, w_up, w_down):
    n_rows = xs.shape[0]
    nblk = n_rows // MOE_BLOCK
    used = lambda g, be, nv, nu, nx: (jnp.minimum(g, nu[0] - 1), 0, 0)
    yout = lambda g, be, nv, nu, nx: (g, 0, 0)
    hbm = pl.BlockSpec(memory_space=pl.ANY)
    return pl.pallas_call(
        _moe_kernel,
        grid_spec=pltpu.PrefetchScalarGridSpec(
            num_scalar_prefetch=4,
            grid=(nblk,),
            in_specs=[pl.BlockSpec((MOE_BLOCK,) + ROW_TILE, used), hbm, hbm, hbm],
            out_specs=pl.BlockSpec((MOE_BLOCK,) + ROW_TILE, yout),
            scratch_shapes=[
                pltpu.VMEM((D_MODEL, D_EXPERT), F32),
                pltpu.VMEM((D_MODEL, D_EXPERT), F32),
                pltpu.VMEM((D_EXPERT, D_MODEL), F32),
                pltpu.VMEM((D_MODEL, D_EXPERT), BF16),
                pltpu.VMEM((D_MODEL, D_EXPERT), BF16),
                pltpu.VMEM((D_EXPERT, D_MODEL), BF16),
                pltpu.SemaphoreType.DMA((3,)),
            ],
        ),
        out_shape=jax.ShapeDtypeStruct(xs.shape, xs.dtype),
        compiler_params=pltpu.CompilerParams(
            dimension_semantics=("arbitrary",), vmem_limit_bytes=VMEM_LIMIT),
        name="moe",
    )(block_e, n_valid, n_used, next_e, xs, w_gate, w_up, w_down)


def _combine_kernel(pos_ref, h_ref, wts_ref, ys_ref, wsg_ref, wsu_ref, wsd_ref, g_ref, b_ref,
                    o_ref, gbuf, acc_ref, sem):
    tc = h_ref.shape[0]

    def issue(t, carry):
        for k in range(TOP_K):
            pltpu.make_async_copy(ys_ref.at[pos_ref[k, t]], gbuf.at[k, t], sem).start(priority=k % 2)
        return carry

    lax.fori_loop(0, tc, issue, 0, unroll=4)

    hb = h_ref[...].reshape(tc, D_MODEL)
    act = _silu(jnp.dot(hb, wsg_ref[...], preferred_element_type=F32))
    act = (act * jnp.dot(hb, wsu_ref[...], preferred_element_type=F32)).astype(BF16)
    shared = jnp.dot(act, wsd_ref[...], preferred_element_type=F32)

    for k in range(TOP_K):
        pltpu.make_async_copy(ys_ref.at[pl.ds(0, tc)], gbuf.at[k], sem).wait()

    def weighted_sum(c, carry):
        r = pl.multiple_of(c * SUBLANES, SUBLANES)
        wts = wts_ref[pl.ds(r, SUBLANES)]
        acc = jnp.zeros((SUBLANES,) + ROW_TILE, F32)
        for k in range(TOP_K):
            wk = jnp.broadcast_to(wts[:, k:k + 1, :], acc.shape)
            acc = acc + wk * gbuf[k, pl.ds(r, SUBLANES)].astype(F32)
        acc_ref[pl.ds(r, SUBLANES), :] = acc.reshape(SUBLANES, D_MODEL)
        return carry

    lax.fori_loop(0, tc // SUBLANES, weighted_sum, 0)
    z = DN_ALPHA * hb.astype(F32) + shared + acc_ref[...]
    o_ref[...] = _layer_norm(z, g_ref[...], b_ref[...])


def _combine(pos3, h1, wts3, ys, wsg, wsu, wsd, g, b, tc):
    n = h1.shape[0]
    tp = pos3.shape[2]
    per = tp // tc
    row3 = lambda i: (i, 0, 0)
    return pl.pallas_call(
        _combine_kernel,
        grid=(n // tc,),
        in_specs=[
            pl.BlockSpec((None, TOP_K, tc), lambda i: (i // per, 0, i % per), memory_space=pltpu.SMEM),
            pl.BlockSpec((tc,) + ROW_TILE, row3),
            pl.BlockSpec((tc,) + wts3.shape[1:], row3),
            pl.BlockSpec(memory_space=pl.ANY),
            _const_spec(wsg.shape), _const_spec(wsu.shape), _const_spec(wsd.shape),
            _const_spec((1, D_MODEL)), _const_spec((1, D_MODEL)),
        ],
        out_specs=pl.BlockSpec((tc, D_MODEL), lambda i: (i, 0)),
        out_shape=jax.ShapeDtypeStruct((n, D_MODEL), F32),
        scratch_shapes=[pltpu.VMEM((TOP_K, tc) + ROW_TILE, BF16), pltpu.VMEM((tc, D_MODEL), F32),
                        pltpu.SemaphoreType.DMA],
        compiler_params=pltpu.CompilerParams(
            dimension_semantics=("arbitrary",), vmem_limit_bytes=VMEM_LIMIT),
        name="combine",
    )(pos3, h1, wts3, ys, wsg, wsu, wsd, g, b)


def _rotate_half_cols(w):
    half = w.shape[-1] // 2
    return jnp.concatenate([-w[..., half:], w[..., :half]], axis=-1)


def _rope_tables(pos):
    inv_freq = ROPE_THETA ** (-jnp.arange(0, B_ROPE, 2, dtype=F32) / B_ROPE)
    ang = pos[:, None] * inv_freq[None, :]
    cos = jnp.tile(jnp.cos(ang), (1, 2 * B_HEADS))
    sin = jnp.tile(jnp.sin(ang), (1, 2 * B_HEADS))
    return cos, sin


def kernel(x, meta_tokens, ln_emb_g, ln_emb_b, w_in, q_norm_g, w_q_up, kv_norm_g, w_kv_up, sink_logits, w_out,
           ln_mix_g, ln_mix_b, w_router, router_bias, w_exp_gate, w_exp_up, w_exp_down, w_sh_gate, w_sh_up,
           w_sh_down, ln_ffn_g, ln_ffn_b):
    batch, seq, d = x.shape
    assert d == D_MODEL and seq % 512 == 0 and w_in.shape[0] == DEPTH
    n = batch * seq
    row2 = lambda v: v.reshape(1, -1).astype(F32)

    w_in0 = w_in[0]
    kpe_cols = w_in0[:, -B_ROPE:]
    w_in_b = jnp.concatenate([w_in0, _rotate_half_cols(kpe_cols)], axis=1).astype(BF16)
    wq = w_q_up[0]
    wq_pe = wq[:, :, B_NOPE:]
    wq_b = jnp.concatenate([
        wq[:, :, :B_NOPE].reshape(B_Q_RANK, -1),
        wq_pe.reshape(B_Q_RANK, -1),
        _rotate_half_cols(wq_pe).reshape(B_Q_RANK, -1)], axis=1).astype(BF16)
    wkv = w_kv_up[0]
    wkv_b = jnp.concatenate([
        wkv[:, :, :B_NOPE].reshape(B_KV_RANK, -1), wkv[:, :, B_NOPE:].reshape(B_KV_RANK, -1)], axis=1).astype(BF16)
    w_out_b = w_out[0].astype(BF16)
    wr_t = w_router[0].astype(F32).T
    wr_hi = wr_t.astype(BF16)
    wr_lo = (wr_t - wr_hi.astype(F32)).astype(BF16)

    cos_m, sin_m = _rope_tables(jnp.arange(N_META, dtype=F32))
    cos_x, sin_x = _rope_tables(jnp.arange(N_META, N_META + seq, dtype=F32))
    rep_col = jnp.arange(A_Q_W)
    rep_src = rep_col // (A_Q_W // A_KV_HEADS) * A_HEAD_DIM + rep_col % A_HEAD_DIM
    rep = (jnp.arange(A_KV_W)[:, None] == rep_src[None, :]).astype(BF16)
    proj_args = (row2(ln_emb_g), row2(ln_emb_b), w_in_b, row2(q_norm_g[0]), wq_b, row2(kv_norm_g[0]), wkv_b, rep)
    _, _, ka_m, va_m, _, km_m, vm_m = _embed_proj(
        meta_tokens.astype(F32), 1, N_META, N_META, *proj_args, cos_m, sin_m)
    h0, qa, ka, va, qm, km, vm = _embed_proj(x.reshape(n, d), batch, seq, 256, *proj_args, cos_x, sin_x)

    pad = jnp.zeros((BLK - N_META, A_Q_W), BF16)
    oa = _win_attn(sink_logits[0].astype(F32), qa, ka, va,
                   jnp.concatenate([pad, ka_m], axis=0), jnp.concatenate([pad, va_m], axis=0), batch, seq)
    ob = _mla_attn(qm, km, vm, km_m, vm_m, min(seq, 1024))

    h1, logits_t = _out_proj(oa, ob, h0, w_out_b[:A_Q_W], w_out_b[A_Q_W:], row2(ln_mix_g[0]), row2(ln_mix_b[0]),
                              wr_hi, wr_lo, 512)

    eidx, wts, rank, counts = _route(logits_t, router_bias[0].astype(F32).reshape(N_EXPERTS, 1), 512)

    counts = counts[:, 0]
    nk = n * TOP_K
    padded = (counts + MOE_BLOCK - 1) // MOE_BLOCK * MOE_BLOCK
    pad_ends = jnp.cumsum(padded)
    pad_starts = pad_ends - padded
    n_rows = (nk + MOE_BLOCK - 1) // MOE_BLOCK * MOE_BLOCK + N_EXPERTS * MOE_BLOCK
    n_blocks = n_rows // MOE_BLOCK
    blk_row0 = jnp.arange(n_blocks, dtype=I32) * MOE_BLOCK
    block_e = jnp.minimum(jnp.sum(pad_ends[None, :] <= blk_row0[:, None], axis=1), N_EXPERTS - 1).astype(I32)
    owner = block_e[:, None] == jnp.arange(N_EXPERTS, dtype=I32)[None, :]
    per_block = lambda table: jnp.sum(jnp.where(owner, table[None, :], 0), axis=1)
    seg_ends = (pad_starts + counts).astype(I32)
    n_valid = jnp.clip(per_block(seg_ends) - blk_row0, 0, MOE_BLOCK).astype(I32)
    n_used = (pad_ends[-1:] // MOE_BLOCK).astype(I32)
    eid = jnp.arange(N_EXPERTS, dtype=I32)
    later_used = jnp.where((eid[None, :] > eid[:, None]) & (counts[None, :] > 0), eid[None, :], N_EXPERTS)
    next_used = jnp.min(later_used, axis=1)
    next_e = per_block(jnp.where(next_used < N_EXPERTS, next_used, -1).astype(I32)).astype(I32)

    pos3 = _positions(eidx, rank, pad_starts.astype(I32).reshape(N_EXPERTS, 1), 512)
    xs = _dispatch(seg_ends, (padded - counts).astype(I32), n_used, pos3, h1, n_rows, 256)
    ys = _moe(block_e, n_valid, n_used, next_e, xs, w_exp_gate.reshape(w_exp_gate.shape[1:]),
              w_exp_up.reshape(w_exp_up.shape[1:]), w_exp_down.reshape(w_exp_down.shape[1:]))
    wts3 = jnp.broadcast_to(wts.T[:, :, None], (n, TOP_K, BLK))
    out = _combine(pos3, h1, wts3, ys, w_sh_gate[0].astype(BF16), w_sh_up[0].astype(BF16),
                   w_sh_down[0].astype(BF16), row2(ln_ffn_g[0]), row2(ln_ffn_b[0]), 128)
    return out.reshape(batch, seq, d)
```

```python
import functools

import jax
import jax.numpy as jnp
from jax import lax
from jax.experimental import pallas as pl
from jax.experimental.pallas import tpu as pltpu

F32 = jnp.float32
BF16 = jnp.bfloat16
I32 = jnp.int32

D_MODEL = 2048
N_META = 16
BLK = 128
A_HEADS = 16
A_KV_HEADS = 4
A_HEAD_DIM = 64
A_WINDOW = 128
B_HEADS = 8
B_Q_RANK = 512
B_KV_RANK = 256
B_NOPE = 128
B_ROPE = 64
B_V = 128
ROPE_THETA = 10000.0
A_Q_W = A_HEADS * A_HEAD_DIM
A_KV_W = A_KV_HEADS * A_HEAD_DIM
N_EXPERTS = 256
N_GROUPS = 8
GROUP_SIZE = N_EXPERTS // N_GROUPS
TOPK_GROUPS = 4
TOP_K = 8
D_EXPERT = 512
D_SHARED = 512
ROUTED_SCALE = 2.5
MOE_BLOCK = 256
DEPTH = 1
DN_ALPHA = (2 * DEPTH) ** 0.25
LN_EPS = 1e-5
RMS_EPS = 1e-6
NEG_INF = -1e30

A_SCALE = A_HEAD_DIM ** -0.5
B_SCALE = (B_NOPE + B_ROPE) ** -0.5
B_QK = B_NOPE + B_ROPE

SUBLANES = 8
LANES = 128
ROW_TILE = (D_MODEL // LANES, LANES)
assert ROW_TILE[0] == 2 * SUBLANES

VMEM_LIMIT = 56 * 1024 * 1024

NT_DIMS = (((1,), (1,)), ((), ()))


def _const_spec(shape):
    zeros = (0,) * len(shape)
    return pl.BlockSpec(shape, lambda *_: zeros, pipeline_mode=pl.Buffered(1))


def _layer_norm(z, g, b):
    mu = jnp.mean(z, axis=-1, keepdims=True)
    zc = z - mu
    var = jnp.mean(zc * zc, axis=-1, keepdims=True)
    return zc * lax.rsqrt(var + LN_EPS) * g + b


def _rms_norm(z, g):
    return z * lax.rsqrt(jnp.mean(z * z, axis=-1, keepdims=True) + RMS_EPS) * g


def _silu(z):
    return z * jax.nn.sigmoid(z)


def _embed_proj_kernel(x_ref, g_ref, b_ref, win_ref, gq_ref, wq_ref, gkv_ref, wkv_ref, cos_ref, sin_ref, rep_ref,
                       h0_ref, qa_ref, ka_ref, va_ref, qm_ref, km_ref, vm_ref):
    h = _layer_norm(x_ref[...], g_ref[...], b_ref[...])
    hb = h.astype(BF16)
    h0_ref[...] = hb
    p = jnp.dot(hb, win_ref[...], preferred_element_type=F32)
    o = 0
    qa_ref[...] = (p[:, o:o + A_Q_W] * A_SCALE).astype(BF16)
    o += A_Q_W
    ka_ref[...] = jnp.dot(p[:, o:o + A_KV_W].astype(BF16), rep_ref[...], preferred_element_type=F32).astype(BF16)
    o += A_KV_W
    va_ref[...] = jnp.dot(p[:, o:o + A_KV_W].astype(BF16), rep_ref[...], preferred_element_type=F32).astype(BF16)
    o += A_KV_W
    cq = p[:, o:o + B_Q_RANK]
    o += B_Q_RANK
    ckv = p[:, o:o + B_KV_RANK]
    o += B_KV_RANK
    kpe = p[:, o:o + B_ROPE]
    o += B_ROPE
    kpe_rot = p[:, o:o + B_ROPE]

    cos = cos_ref[...]
    sin = sin_ref[...]
    qall = jnp.dot(_rms_norm(cq, gq_ref[...]).astype(BF16), wq_ref[...], preferred_element_type=F32)
    n_nope = B_HEADS * B_NOPE
    n_pe = B_HEADS * B_ROPE
    q_pe = qall[:, n_nope:n_nope + n_pe] * cos + qall[:, n_nope + n_pe:] * sin
    kv = jnp.dot(_rms_norm(ckv, gkv_ref[...]).astype(BF16), wkv_ref[...], preferred_element_type=F32)
    k_pe = (kpe * cos[:, :B_ROPE] + kpe_rot * sin[:, :B_ROPE]).astype(BF16)
    for hd in range(B_HEADS):
        qn = (qall[:, hd * B_NOPE:(hd + 1) * B_NOPE] * B_SCALE).astype(BF16)
        qp = (q_pe[:, hd * B_ROPE:(hd + 1) * B_ROPE] * B_SCALE).astype(BF16)
        qm_ref[hd] = jnp.concatenate([qn, qp], axis=1)
        kn = kv[:, hd * B_NOPE:(hd + 1) * B_NOPE].astype(BF16)
        km_ref[hd] = jnp.concatenate([kn, k_pe], axis=1)
        vm_ref[hd] = kv[:, n_nope + hd * B_V:n_nope + (hd + 1) * B_V].astype(BF16)


def _embed_proj(x2, batch, seq, tm, ln_g, ln_b, w_in, gq, wq, gkv, wkv, rep, cos, sin):
    n = batch * seq
    nt = seq // tm
    row = lambda b, i: (b * nt + i, 0)
    head_major = lambda b, i: (b, 0, i, 0)
    in_w = w_in.shape[1]
    return pl.pallas_call(
        _embed_proj_kernel,
        grid=(batch, nt),
        in_specs=[
            pl.BlockSpec((tm, D_MODEL), row),
            _const_spec((1, D_MODEL)), _const_spec((1, D_MODEL)),
            _const_spec((D_MODEL, in_w)),
            _const_spec((1, B_Q_RANK)), _const_spec(wq.shape),
            _const_spec((1, B_KV_RANK)), _const_spec(wkv.shape),
            pl.BlockSpec((tm, B_HEADS * B_ROPE), lambda b, i: (i, 0)),
            pl.BlockSpec((tm, B_HEADS * B_ROPE), lambda b, i: (i, 0)),
            _const_spec(rep.shape),
        ],
        out_specs=[
            pl.BlockSpec((tm, D_MODEL), row),
            pl.BlockSpec((tm, A_Q_W), row),
            pl.BlockSpec((tm, A_Q_W), row),
            pl.BlockSpec((tm, A_Q_W), row),
            pl.BlockSpec((None, B_HEADS, tm, B_QK), head_major),
            pl.BlockSpec((None, B_HEADS, tm, B_QK), head_major),
            pl.BlockSpec((None, B_HEADS, tm, B_V), head_major),
        ],
        out_shape=[
            jax.ShapeDtypeStruct((n, D_MODEL), BF16),
            jax.ShapeDtypeStruct((n, A_Q_W), BF16),
            jax.ShapeDtypeStruct((n, A_Q_W), BF16),
            jax.ShapeDtypeStruct((n, A_Q_W), BF16),
            jax.ShapeDtypeStruct((batch, B_HEADS, seq, B_QK), BF16),
            jax.ShapeDtypeStruct((batch, B_HEADS, seq, B_QK), BF16),
            jax.ShapeDtypeStruct((batch, B_HEADS, seq, B_V), BF16),
        ],
        compiler_params=pltpu.CompilerParams(
            dimension_semantics=("arbitrary", "arbitrary"), vmem_limit_bytes=VMEM_LIMIT),
        name="embed_proj",
    )(x2, ln_g, ln_b, w_in, gq, wq, gkv, wkv, cos, sin, rep)


def _win_attn_kernel(sink_ref, q_ref, kp_ref, kc_ref, kn_ref, vp_ref, vc_ref, vn_ref, kmeta_ref, vmeta_ref,
                     bias_ref, o_ref):
    first = pl.program_id(1) == 0
    grp = A_HEADS // A_KV_HEADS
    gw = grp * A_HEAD_DIM
    k3 = jnp.concatenate([jnp.where(first, kmeta_ref[...], kp_ref[...]), kc_ref[...], kn_ref[...]], axis=0)
    v3 = jnp.concatenate([jnp.where(first, vmeta_ref[...], vp_ref[...]), vc_ref[...], vn_ref[...]], axis=0)
    lane_head = lax.broadcasted_iota(I32, (BLK, gw), 1) // A_HEAD_DIM
    row_head = lax.broadcasted_iota(I32, (grp * BLK, 1), 0) // BLK
    for kvh in range(A_KV_HEADS):
        cols = slice(kvh * gw, (kvh + 1) * gw)
        qg = q_ref[:, cols]
        q4 = jnp.concatenate([jnp.where(lane_head == r, qg, jnp.zeros_like(qg)) for r in range(grp)], axis=0)
        sink = jnp.zeros((grp * BLK, 1), F32)
        for r in range(grp):
            sink = jnp.where(row_head == r, sink_ref[kvh * grp + r], sink)
        s = lax.dot_general(q4, k3[:, cols], NT_DIMS, preferred_element_type=F32) + bias_ref[kvh]
        m = jnp.maximum(jnp.max(s, axis=-1, keepdims=True), sink)
        p = jnp.exp(s - m)
        denom = jnp.sum(p, axis=-1, keepdims=True) + jnp.exp(sink - m)
        ow = jnp.dot(p.astype(BF16), v3[:, cols], preferred_element_type=F32) / denom
        og = jnp.zeros((BLK, gw), F32)
        for r in range(grp):
            og = jnp.where(lane_head == r, ow[r * BLK:(r + 1) * BLK], og)
        o_ref[:, cols] = og.astype(BF16)


def _win_bias():
    qi = jnp.arange(BLK)[:, None]
    ki = jnp.arange(3 * BLK)[None, :]
    dist = jnp.abs(BLK + qi - ki)
    in_window = dist <= A_WINDOW
    valid = jnp.stack([in_window, in_window & (ki >= BLK - N_META), in_window & (ki < 2 * BLK)])
    slopes = jnp.exp2(-8.0 * jnp.arange(1, A_HEADS + 1, dtype=F32) / A_HEADS)
    bias = jnp.where(valid[:, None], -slopes[None, :, None, None] * dist.astype(F32)[None, None], NEG_INF)
    return bias.reshape(3, A_KV_HEADS, (A_HEADS // A_KV_HEADS) * BLK, 3 * BLK)


def _win_attn(sink, qa, ka, va, kmeta, vmeta, batch, seq):
    nb = seq // BLK
    n = batch * seq
    bias = _win_bias()
    cur = lambda b, i, *_: (b * nb + i, 0)
    prev = lambda b, i, *_: (b * nb + jnp.maximum(i - 1, 0), 0)
    nxt = lambda b, i, *_: (b * nb + jnp.minimum(i + 1, nb - 1), 0)
    kv_spec = lambda f: pl.BlockSpec((BLK, A_Q_W), f)
    const = lambda b, i, *_: (0, 0)
    variant = lambda b, i, *_: (jnp.where(i == 0, 1, jnp.where(i == nb - 1, 2, 0)), 0, 0, 0)
    return pl.pallas_call(
        _win_attn_kernel,
        grid_spec=pltpu.PrefetchScalarGridSpec(
            num_scalar_prefetch=1,
            grid=(batch, nb),
            in_specs=[
                pl.BlockSpec((BLK, A_Q_W), cur),
                kv_spec(prev), kv_spec(cur), kv_spec(nxt),
                kv_spec(prev), kv_spec(cur), kv_spec(nxt),
                pl.BlockSpec((BLK, A_Q_W), const), pl.BlockSpec((BLK, A_Q_W), const),
                pl.BlockSpec((None,) + bias.shape[1:], variant),
            ],
            out_specs=pl.BlockSpec((BLK, A_Q_W), cur),
        ),
        out_shape=jax.ShapeDtypeStruct((n, A_Q_W), BF16),
        compiler_params=pltpu.CompilerParams(
            dimension_semantics=("arbitrary", "arbitrary"), vmem_limit_bytes=VMEM_LIMIT),
        name="win_attn",
    )(sink, qa, ka, ka, ka, va, va, va, kmeta, vmeta, bias)


MLA_KEY_CHUNKS = 2


def _mla_attn_kernel(q_ref, k_ref, v_ref, km_ref, vm_ref, o_ref):
    q = q_ref[...]
    sm = lax.dot_general(q, km_ref[...], NT_DIMS, preferred_element_type=F32)
    m = jnp.max(sm, axis=-1, keepdims=True)
    pm = jnp.exp(sm - m)
    denom = jnp.sum(pm, axis=-1, keepdims=True)
    acc = jnp.dot(pm.astype(BF16), vm_ref[...], preferred_element_type=F32)
    chunk = k_ref.shape[0] // MLA_KEY_CHUNKS
    for c in range(MLA_KEY_CHUNKS):
        keys = slice(c * chunk, (c + 1) * chunk)
        s = lax.dot_general(q, k_ref[keys, :], NT_DIMS, preferred_element_type=F32)
        m_new = jnp.maximum(m, jnp.max(s, axis=-1, keepdims=True))
        rescale = jnp.exp(m - m_new)
        p = jnp.exp(s - m_new)
        denom = rescale * denom + jnp.sum(p, axis=-1, keepdims=True)
        acc = rescale * acc + jnp.dot(p.astype(BF16), v_ref[keys, :], preferred_element_type=F32)
        m = m_new
    o_ref[...] = (acc / denom).astype(BF16)


def _mla_attn(qm, km, vm, km_meta, vm_meta, tq):
    batch, heads, seq, _ = qm.shape
    nq = seq // tq
    return pl.pallas_call(
        _mla_attn_kernel,
        grid=(batch, heads, nq),
        in_specs=[
            pl.BlockSpec((None, None, tq, B_QK), lambda b, h, i: (b, h, i, 0)),
            pl.BlockSpec((None, None, seq, B_QK), lambda b, h, i: (b, h, 0, 0)),
            pl.BlockSpec((None, None, seq, B_V), lambda b, h, i: (b, h, 0, 0)),
            pl.BlockSpec((None, None, N_META, B_QK), lambda b, h, i: (0, h, 0, 0)),
            pl.BlockSpec((None, None, N_META, B_V), lambda b, h, i: (0, h, 0, 0)),
        ],
        out_specs=pl.BlockSpec((tq, B_V), lambda b, h, i: (b * nq + i, h)),
        out_shape=jax.ShapeDtypeStruct((batch * seq, heads * B_V), BF16),
        compiler_params=pltpu.CompilerParams(
            dimension_semantics=("arbitrary", "arbitrary", "arbitrary"), vmem_limit_bytes=VMEM_LIMIT),
        name="mla_attn",
    )(qm, km, vm, km_meta, vm_meta)


def _out_proj_kernel(oa_ref, ob_ref, h0_ref, wa_ref, wb_ref, g_ref, b_ref, wr_hi_ref, wr_lo_ref,
                     h1_ref, lg_ref):
    mix = jnp.dot(oa_ref[...], wa_ref[...], preferred_element_type=F32)
    mix = mix + jnp.dot(ob_ref[...], wb_ref[...], preferred_element_type=F32)
    h1 = _layer_norm(DN_ALPHA * h0_ref[...].astype(F32) + mix, g_ref[...], b_ref[...])
    hi = h1.astype(BF16)
    hi_f = hi.astype(F32)
    lo = (h1 - hi_f).astype(BF16)
    lg = lax.dot_general(wr_hi_ref[...], hi, NT_DIMS, preferred_element_type=F32)
    lg = lg + lax.dot_general(wr_hi_ref[...], lo, NT_DIMS, preferred_element_type=F32)
    lg = lg + lax.dot_general(wr_lo_ref[...], hi, NT_DIMS, preferred_element_type=F32)
    lg_ref[...] = lg
    h1_ref[...] = hi.reshape(h1_ref.shape)


def _out_proj(oa, ob, h0, wa, wb, g, b, wr_hi, wr_lo, tm):
    n = oa.shape[0]
    row = lambda i: (i, 0)
    return pl.pallas_call(
        _out_proj_kernel,
        grid=(n // tm,),
        in_specs=[
            pl.BlockSpec((tm, A_Q_W), row),
            pl.BlockSpec((tm, B_HEADS * B_V), row),
            pl.BlockSpec((tm, D_MODEL), row),
            _const_spec(wa.shape), _const_spec(wb.shape),
            _const_spec((1, D_MODEL)), _const_spec((1, D_MODEL)),
            _const_spec(wr_hi.shape), _const_spec(wr_lo.shape),
        ],
        out_specs=[
            pl.BlockSpec((tm,) + ROW_TILE, lambda i: (i, 0, 0)),
            pl.BlockSpec((N_EXPERTS, tm), lambda i: (0, i)),
        ],
        out_shape=[
            jax.ShapeDtypeStruct((n,) + ROW_TILE, BF16),
            jax.ShapeDtypeStruct((N_EXPERTS, n), F32),
        ],
        compiler_params=pltpu.CompilerParams(
            dimension_semantics=("arbitrary",), vmem_limit_bytes=VMEM_LIMIT),
        name="out_proj",
    )(oa, ob, h0, wa, wb, g, b, wr_hi, wr_lo)


def _first_max(cur, idx, sentinel):
    m = jnp.max(cur, axis=0, keepdims=True)
    am = jnp.min(jnp.where(cur == m, idx, sentinel), axis=0, keepdims=True)
    return m, am


def _route_kernel(lg_ref, bias_ref, eidx_ref, wts_ref, rank_ref, cnt_ref, carry_ref):
    @pl.when(pl.program_id(0) == 0)
    def _():
        carry_ref[...] = jnp.zeros_like(carry_ref)

    t = lg_ref.shape[1]
    scores = jax.nn.sigmoid(lg_ref[...])
    sel = scores + bias_ref[...]
    neg_inf = jnp.float32(-jnp.inf)

    r32 = lax.broadcasted_iota(I32, (GROUP_SIZE, t), 0)
    gs = []
    for g in range(N_GROUPS):
        blk = sel[g * GROUP_SIZE:(g + 1) * GROUP_SIZE]
        m1, i1 = _first_max(blk, r32, GROUP_SIZE)
        m2 = jnp.max(jnp.where(r32 == i1, neg_inf, blk), axis=0, keepdims=True)
        gs.append(m1 + m2)
    cur = jnp.concatenate(gs, axis=0)
    r8 = lax.broadcasted_iota(I32, (N_GROUPS, t), 0)
    keep = r8 < 0
    for _ in range(TOPK_GROUPS):
        _, ig = _first_max(cur, r8, N_GROUPS)
        hit = r8 == ig
        keep = keep | hit
        cur = jnp.where(hit, neg_inf, cur)
    cur = jnp.concatenate(
        [jnp.where(keep[g:g + 1], sel[g * GROUP_SIZE:(g + 1) * GROUP_SIZE], NEG_INF) for g in range(N_GROUPS)],
        axis=0)

    row = lax.broadcasted_iota(I32, (N_EXPERTS, t), 0)
    chosen = jnp.zeros((N_EXPERTS, t), F32)
    idxs, ws = [], []
    for _ in range(TOP_K):
        _, ik = _first_max(cur, row, N_EXPERTS)
        hit = row == ik
        ws.append(jnp.sum(jnp.where(hit, scores, 0.0), axis=0, keepdims=True))
        cur = jnp.where(hit, neg_inf, cur)
        chosen = jnp.where(hit, 1.0, chosen)
        idxs.append(ik)
    eidx = jnp.concatenate(idxs, axis=0)
    w = jnp.concatenate(ws, axis=0)
    eidx_ref[...] = eidx
    wts_ref[...] = w / jnp.sum(w, axis=0, keepdims=True) * ROUTED_SCALE

    before = lax.broadcasted_iota(I32, (t, t), 0) < lax.broadcasted_iota(I32, (t, t), 1)
    prefix = jnp.dot(chosen.astype(BF16), jnp.where(before, 1.0, 0.0).astype(BF16), preferred_element_type=F32)
    prefix = prefix + carry_ref[...]
    ranks = [jnp.sum(jnp.where(row == idxs[k], prefix, 0.0), axis=0, keepdims=True) for k in range(TOP_K)]
    rank_ref[...] = jnp.concatenate(ranks, axis=0).astype(I32)
    carry_ref[...] = carry_ref[...] + jnp.sum(chosen, axis=1, keepdims=True)
    cnt_ref[...] = carry_ref[...].astype(I32)


def _route(logits_t, bias, t):
    n = logits_t.shape[1]
    col = lambda i: (0, i)
    return pl.pallas_call(
        _route_kernel,
        grid=(n // t,),
        in_specs=[pl.BlockSpec((N_EXPERTS, t), col), _const_spec((N_EXPERTS, 1))],
        out_specs=[
            pl.BlockSpec((TOP_K, t), col), pl.BlockSpec((TOP_K, t), col), pl.BlockSpec((TOP_K, t), col),
            pl.BlockSpec((N_EXPERTS, 1), lambda i: (0, 0)),
        ],
        out_shape=[
            jax.ShapeDtypeStruct((TOP_K, n), I32),
            jax.ShapeDtypeStruct((TOP_K, n), F32),
            jax.ShapeDtypeStruct((TOP_K, n), I32),
            jax.ShapeDtypeStruct((N_EXPERTS, 1), I32),
        ],
        scratch_shapes=[pltpu.VMEM((N_EXPERTS, 1), F32)],
        compiler_params=pltpu.CompilerParams(
            dimension_semantics=("arbitrary",), vmem_limit_bytes=VMEM_LIMIT),
        name="route",
    )(logits_t, bias)


def _positions_kernel(eidx_ref, rank_ref, start_ref, pos_ref):
    t = eidx_ref.shape[1]
    row = lax.broadcasted_iota(I32, (N_EXPERTS, t), 0)
    start = start_ref[...]
    eidx = eidx_ref[...]
    base = [jnp.sum(jnp.where(row == eidx[k:k + 1], start, 0), axis=0, keepdims=True) for k in range(TOP_K)]
    pos_ref[...] = jnp.concatenate(base, axis=0) + rank_ref[...]


def _positions(eidx, rank, starts, t):
    n = eidx.shape[1]
    col = lambda i: (0, i)
    return pl.pallas_call(
        _positions_kernel,
        grid=(n // t,),
        in_specs=[pl.BlockSpec((TOP_K, t), col), pl.BlockSpec((TOP_K, t), col), _const_spec((N_EXPERTS, 1))],
        out_specs=pl.BlockSpec((None, TOP_K, t), lambda i: (i, 0, 0)),
        out_shape=jax.ShapeDtypeStruct((n // t, TOP_K, t), I32),
        compiler_params=pltpu.CompilerParams(dimension_semantics=("arbitrary",)),
        name="positions",
    )(eidx, rank, starts)


def _zero_fill(row0_ref, len_ref, nu_ref, zbuf, xs_ref, sem, wait):
    def run(copy):
        copy.wait() if wait else copy.start()

    def per_expert(e, carry):
        r0 = row0_ref[e]
        ln = len_ref[e]
        bit = MOE_BLOCK // 2
        while bit >= 1:
            off = ln - (ln & (2 * bit - 1))

            @pl.when((ln & bit) != 0)
            def _(bit=bit, off=off):
                run(pltpu.make_async_copy(zbuf.at[pl.ds(0, bit)], xs_ref.at[pl.ds(r0 + off, bit)], sem))
            bit //= 2
        return carry

    lax.fori_loop(0, N_EXPERTS, per_expert, 0)

    def per_block(g, carry):
        run(pltpu.make_async_copy(zbuf, xs_ref.at[pl.ds(g * MOE_BLOCK, MOE_BLOCK)], sem))
        return carry

    lax.fori_loop(nu_ref[0], xs_ref.shape[0] // MOE_BLOCK, per_block, 0)


def _dispatch_kernel(row0_ref, len_ref, nu_ref, pos_ref, h_ref, xs_ref, zbuf, sem, zsem):
    td = h_ref.shape[0]

    @pl.when(pl.program_id(0) == 0)
    def _():
        zbuf[...] = jnp.zeros_like(zbuf)
        _zero_fill(row0_ref, len_ref, nu_ref, zbuf, xs_ref, zsem, wait=False)

    def issue(t, carry):
        for k in range(TOP_K):
            pltpu.make_async_copy(h_ref.at[t], xs_ref.at[pos_ref[k, t]], sem).start(priority=k % 2)
        return carry

    lax.fori_loop(0, td, issue, 0, unroll=4)
    for _ in range(TOP_K):
        pltpu.make_async_copy(h_ref, xs_ref.at[pl.ds(0, td)], sem).wait()

    @pl.when(pl.program_id(0) == pl.num_programs(0) - 1)
    def _():
        _zero_fill(row0_ref, len_ref, nu_ref, zbuf, xs_ref, zsem, wait=True)


def _dispatch(pad_row0, pad_len, n_used, pos3, h1, n_rows, td):
    n = h1.shape[0]
    tp = pos3.shape[2]
    per = tp // td
    return pl.pallas_call(
        _dispatch_kernel,
        grid_spec=pltpu.PrefetchScalarGridSpec(
            num_scalar_prefetch=3,
            grid=(n // td,),
            in_specs=[
                pl.BlockSpec((None, TOP_K, td), lambda i, *_: (i // per, 0, i % per), memory_space=pltpu.SMEM),
                pl.BlockSpec((td,) + ROW_TILE, lambda i, *_: (i, 0, 0)),
            ],
            out_specs=pl.BlockSpec(memory_space=pl.ANY),
            scratch_shapes=[pltpu.VMEM((MOE_BLOCK,) + ROW_TILE, BF16), pltpu.SemaphoreType.DMA,
                            pltpu.SemaphoreType.DMA],
        ),
        out_shape=jax.ShapeDtypeStruct((n_rows,) + ROW_TILE, BF16),
        compiler_params=pltpu.CompilerParams(dimension_semantics=("arbitrary",)),
        name="dispatch",
    )(pad_row0, pad_len, n_used, pos3, h1)


CAST_CHUNK_ELEMS = 32 * 1024


def _cast_rows(src, dst):
    chunk = CAST_CHUNK_ELEMS // src.shape[1]
    for r in range(0, src.shape[0], chunk):
        dst[r:r + chunk, :] = src[r:r + chunk, :].astype(dst.dtype)


def _moe_kernel(be_ref, nv_ref, nu_ref, nxt_ref, x_ref, wg_hbm, wu_hbm, wd_hbm, y_ref,
                wg_land, wu_land, wd_land, wg_s, wu_s, wd_s, sems):
    g = pl.program_id(0)
    nv = nv_ref[g]
    expert = be_ref[g]
    new_expert = (g == 0) | (expert != be_ref[jnp.maximum(g - 1, 0)])

    def weight_copies(e):
        return (pltpu.make_async_copy(wg_hbm.at[e], wg_land, sems.at[0]),
                pltpu.make_async_copy(wu_hbm.at[e], wu_land, sems.at[1]),
                pltpu.make_async_copy(wd_hbm.at[e], wd_land, sems.at[2]))

    @pl.when(g == 0)
    def _():
        for c in weight_copies(expert):
            c.start()

    def expert_mlp():
        x = x_ref[...].reshape(MOE_BLOCK, D_MODEL)
        gate = jnp.dot(x, wg_s[...], preferred_element_type=F32)
        up = jnp.dot(x, wu_s[...], preferred_element_type=F32)
        act = (_silu(gate) * up).astype(BF16)
        y = jnp.dot(act, wd_s[...], preferred_element_type=F32)
        y_ref[...] = y.astype(BF16).reshape(y_ref.shape)

    @pl.when((nv > 0) & new_expert)
    def _():
        for c in weight_copies(expert):
            c.wait()
        _cast_rows(wg_land, wg_s)
        _cast_rows(wu_land, wu_s)
        _cast_rows(wd_land, wd_s)
        expert_mlp()
        nxt = nxt_ref[g]

        @pl.when(nxt >= 0)
        def _():
            for c in weight_copies(nxt):
                c.start()

    @pl.when((nv > 0) & jnp.logical_not(new_expert))
    def _():
        expert_mlp()

    @pl.when(nv <= 0)
    def _():
        y_ref[...] = jnp.zeros_like(y_ref)


def _moe(block_e, n_valid, n_used, next_e, xs, w_gate, w_up, w_down):
    n_rows = xs.shape[0]
    nblk = n_rows // MOE_BLOCK
    used = lambda g, be, nv, nu, nx: (jnp.minimum(g, nu[0] - 1), 0, 0)
    yout = lambda g, be, nv, nu, nx: (g, 0, 0)
    hbm = pl.BlockSpec(memory_space=pl.ANY)
    return pl.pallas_call(
        _moe_kernel,
        grid_spec=pltpu.PrefetchScalarGridSpec(
            num_scalar_prefetch=4,
            grid=(nblk,),
            in_specs=[pl.BlockSpec((MOE_BLOCK,) + ROW_TILE, used), hbm, hbm, hbm],
            out_specs=pl.BlockSpec((MOE_BLOCK,) + ROW_TILE, yout),
            scratch_shapes=[
                pltpu.VMEM((D_MODEL, D_EXPERT), F32),
                pltpu.VMEM((D_MODEL, D_EXPERT), F32),
                pltpu.VMEM((D_EXPERT, D_MODEL), F32),
                pltpu.VMEM((D_MODEL, D_EXPERT), BF16),
                pltpu.VMEM((D_MODEL, D_EXPERT), BF16),
                pltpu.VMEM((D_EXPERT, D_MODEL), BF16),
                pltpu.SemaphoreType.DMA((3,)),
            ],
        ),
        out_shape=jax.ShapeDtypeStruct(xs.shape, xs.dtype),
        compiler_params=pltpu.CompilerParams(
            dimension_semantics=("arbitrary",), vmem_limit_bytes=VMEM_LIMIT),
        name="moe",
    )(block_e, n_valid, n_used, next_e, xs, w_gate, w_up, w_down)


def _combine_kernel(pos_ref, nxt_ref, h_ref, wts_ref, ys_ref, wsg_ref, wsu_ref, wsd_ref, g_ref, b_ref,
                    o_ref, gbuf_a, gbuf_b, acc_ref, sem_a, sem_b):
    tc = h_ref.shape[0] // 2
    step = pl.program_id(0)

    def issue(pos, t0, gbuf, sem):
        for t in range(tc):
            for k in range(TOP_K):
                pltpu.make_async_copy(ys_ref.at[pos[k, t0 + t]], gbuf.at[k, t], sem).start(priority=k % 2)

    def drain(gbuf, sem):
        for k in range(TOP_K):
            pltpu.make_async_copy(ys_ref.at[pl.ds(0, tc)], gbuf.at[k], sem).wait()

    def shared_expert(rows):
        hb = h_ref[rows].reshape(tc, D_MODEL)
        act = _silu(jnp.dot(hb, wsg_ref[...], preferred_element_type=F32))
        act = (act * jnp.dot(hb, wsu_ref[...], preferred_element_type=F32)).astype(BF16)
        return DN_ALPHA * hb.astype(F32) + jnp.dot(act, wsd_ref[...], preferred_element_type=F32)

    def finish(rows, base, gbuf):
        for r in range(0, tc, SUBLANES):
            wts = wts_ref[rows.start + r:rows.start + r + SUBLANES]
            acc = jnp.zeros((SUBLANES,) + ROW_TILE, F32)
            for k in range(TOP_K):
                wk = jnp.broadcast_to(wts[:, k:k + 1, :], acc.shape)
                acc = acc + wk * gbuf[k, r:r + SUBLANES].astype(F32)
            acc_ref[r:r + SUBLANES, :] = acc.reshape(SUBLANES, D_MODEL)
        o_ref[rows, :] = _layer_norm(base + acc_ref[...], g_ref[...], b_ref[...])

    first, second = slice(0, tc), slice(tc, 2 * tc)

    @pl.when(step == 0)
    def _():
        issue(pos_ref, 0, gbuf_a, sem_a)

    drain(gbuf_a, sem_a)
    finish(first, shared_expert(first), gbuf_a)
    issue(pos_ref, tc, gbuf_b, sem_b)

    drain(gbuf_b, sem_b)
    finish(second, shared_expert(second), gbuf_b)
    issue(nxt_ref, 0, gbuf_a, sem_a)

    @pl.when(step == pl.num_programs(0) - 1)
    def _():
        drain(gbuf_a, sem_a)


def _combine(pos3, h1, wts3, ys, wsg, wsu, wsd, g, b, tc):
    n = h1.shape[0]
    tp = pos3.shape[2]
    per = tp // tc
    steps = n // (2 * tc)
    row3 = lambda i: (i, 0, 0)

    def next_tile(i):
        t = jnp.minimum(2 * i + 2, 2 * steps - 1)
        return (t // per, 0, t % per)

    return pl.pallas_call(
        _combine_kernel,
        grid=(steps,),
        in_specs=[
            pl.BlockSpec((None, TOP_K, 2 * tc), lambda i: (2 * i // per, 0, i % (per // 2)), memory_space=pltpu.SMEM),
            pl.BlockSpec((None, TOP_K, tc), next_tile, memory_space=pltpu.SMEM),
            pl.BlockSpec((2 * tc,) + ROW_TILE, row3),
            pl.BlockSpec((2 * tc,) + wts3.shape[1:], row3),
            pl.BlockSpec(memory_space=pl.ANY),
            _const_spec(wsg.shape), _const_spec(wsu.shape), _const_spec(wsd.shape),
            _const_spec((1, D_MODEL)), _const_spec((1, D_MODEL)),
        ],
        out_specs=pl.BlockSpec((2 * tc, D_MODEL), lambda i: (i, 0)),
        out_shape=jax.ShapeDtypeStruct((n, D_MODEL), F32),
        scratch_shapes=[pltpu.VMEM((TOP_K, tc) + ROW_TILE, BF16), pltpu.VMEM((TOP_K, tc) + ROW_TILE, BF16),
                        pltpu.VMEM((tc, D_MODEL), F32), pltpu.SemaphoreType.DMA, pltpu.SemaphoreType.DMA],
        compiler_params=pltpu.CompilerParams(
            dimension_semantics=("arbitrary",), vmem_limit_bytes=VMEM_LIMIT),
        name="combine",
    )(pos3, pos3, h1, wts3, ys, wsg, wsu, wsd, g, b)


def _rotate_half_cols(w):
    half = w.shape[-1] // 2
    return jnp.concatenate([-w[..., half:], w[..., :half]], axis=-1)


def _rope_tables(pos):
    inv_freq = ROPE_THETA ** (-jnp.arange(0, B_ROPE, 2, dtype=F32) / B_ROPE)
    ang = pos[:, None] * inv_freq[None, :]
    cos = jnp.tile(jnp.cos(ang), (1, 2 * B_HEADS))
    sin = jnp.tile(jnp.sin(ang), (1, 2 * B_HEADS))
    return cos, sin


def kernel(x, meta_tokens, ln_emb_g, ln_emb_b, w_in, q_norm_g, w_q_up, kv_norm_g, w_kv_up, sink_logits, w_out,
           ln_mix_g, ln_mix_b, w_router, router_bias, w_exp_gate, w_exp_up, w_exp_down, w_sh_gate, w_sh_up,
           w_sh_down, ln_ffn_g, ln_ffn_b):
    batch, seq, d = x.shape
    assert d == D_MODEL and seq % 512 == 0 and w_in.shape[0] == DEPTH
    n = batch * seq
    row2 = lambda v: v.reshape(1, -1).astype(F32)

    w_in0 = w_in[0]
    kpe_cols = w_in0[:, -B_ROPE:]
    w_in_b = jnp.concatenate([w_in0, _rotate_half_cols(kpe_cols)], axis=1).astype(BF16)
    wq = w_q_up[0]
    wq_pe = wq[:, :, B_NOPE:]
    wq_b = jnp.concatenate([
        wq[:, :, :B_NOPE].reshape(B_Q_RANK, -1),
        wq_pe.reshape(B_Q_RANK, -1),
        _rotate_half_cols(wq_pe).reshape(B_Q_RANK, -1)], axis=1).astype(BF16)
    wkv = w_kv_up[0]
    wkv_b = jnp.concatenate([
        wkv[:, :, :B_NOPE].reshape(B_KV_RANK, -1), wkv[:, :, B_NOPE:].reshape(B_KV_RANK, -1)], axis=1).astype(BF16)
    w_out_b = w_out[0].astype(BF16)
    wr_t = w_router[0].astype(F32).T
    wr_hi = wr_t.astype(BF16)
    wr_lo = (wr_t - wr_hi.astype(F32)).astype(BF16)

    cos_m, sin_m = _rope_tables(jnp.arange(N_META, dtype=F32))
    cos_x, sin_x = _rope_tables(jnp.arange(N_META, N_META + seq, dtype=F32))
    rep_col = jnp.arange(A_Q_W)
    rep_src = rep_col // (A_Q_W // A_KV_HEADS) * A_HEAD_DIM + rep_col % A_HEAD_DIM
    rep = (jnp.arange(A_KV_W)[:, None] == rep_src[None, :]).astype(BF16)
    proj_args = (row2(ln_emb_g), row2(ln_emb_b), w_in_b, row2(q_norm_g[0]), wq_b, row2(kv_norm_g[0]), wkv_b, rep)
    _, _, ka_m, va_m, _, km_m, vm_m = _embed_proj(
        meta_tokens.astype(F32), 1, N_META, N_META, *proj_args, cos_m, sin_m)
    h0, qa, ka, va, qm, km, vm = _embed_proj(x.reshape(n, d), batch, seq, 256, *proj_args, cos_x, sin_x)

    pad = jnp.zeros((BLK - N_META, A_Q_W), BF16)
    oa = _win_attn(sink_logits[0].astype(F32), qa, ka, va,
                   jnp.concatenate([pad, ka_m], axis=0), jnp.concatenate([pad, va_m], axis=0), batch, seq)
    ob = _mla_attn(qm, km, vm, km_m, vm_m, min(seq, 1024))

    h1, logits_t = _out_proj(oa, ob, h0, w_out_b[:A_Q_W], w_out_b[A_Q_W:], row2(ln_mix_g[0]), row2(ln_mix_b[0]),
                              wr_hi, wr_lo, 512)

    eidx, wts, rank, counts = _route(logits_t, router_bias[0].astype(F32).reshape(N_EXPERTS, 1), 512)

    counts = counts[:, 0]
    nk = n * TOP_K
    padded = (counts + MOE_BLOCK - 1) // MOE_BLOCK * MOE_BLOCK
    pad_ends = jnp.cumsum(padded)
    pad_starts = pad_ends - padded
    n_rows = (nk + MOE_BLOCK - 1) // MOE_BLOCK * MOE_BLOCK + N_EXPERTS * MOE_BLOCK
    n_blocks = n_rows // MOE_BLOCK
    blk_row0 = jnp.arange(n_blocks, dtype=I32) * MOE_BLOCK
    block_e = jnp.minimum(jnp.sum(pad_ends[None, :] <= blk_row0[:, None], axis=1), N_EXPERTS - 1).astype(I32)
    owner = block_e[:, None] == jnp.arange(N_EXPERTS, dtype=I32)[None, :]
    per_block = lambda table: jnp.sum(jnp.where(owner, table[None, :], 0), axis=1)
    seg_ends = (pad_starts + counts).astype(I32)
    n_valid = jnp.clip(per_block(seg_ends) - blk_row0, 0, MOE_BLOCK).astype(I32)
    n_used = (pad_ends[-1:] // MOE_BLOCK).astype(I32)
    eid = jnp.arange(N_EXPERTS, dtype=I32)
    later_used = jnp.where((eid[None, :] > eid[:, None]) & (counts[None, :] > 0), eid[None, :], N_EXPERTS)
    next_used = jnp.min(later_used, axis=1)
    next_e = per_block(jnp.where(next_used < N_EXPERTS, next_used, -1).astype(I32)).astype(I32)

    pos3 = _positions(eidx, rank, pad_starts.astype(I32).reshape(N_EXPERTS, 1), 512)
    xs = _dispatch(seg_ends, (padded - counts).astype(I32), n_used, pos3, h1, n_rows, 256)
    ys = _moe(block_e, n_valid, n_used, next_e, xs, w_exp_gate.reshape(w_exp_gate.shape[1:]),
              w_exp_up.reshape(w_exp_up.shape[1:]), w_exp_down.reshape(w_exp_down.shape[1:]))
    wts3 = jnp.broadcast_to(wts.T[:, :, None], (n, TOP_K, BLK))
    out = _combine(pos3, h1, wts3, ys, w_sh_gate[0].astype(BF16), w_sh_up[0].astype(BF16),
                   w_sh_down[0].astype(BF16), row2(ln_ffn_g[0]), row2(ln_ffn_b[0]), 128)
    return out.reshape(batch, seq, d)
```

```python
import functools

import jax
import jax.numpy as jnp
from jax import lax
from jax.experimental import pallas as pl
from jax.experimental.pallas import tpu as pltpu

F32 = jnp.float32
BF16 = jnp.bfloat16
I32 = jnp.int32

D_MODEL = 2048
N_META = 16
BLK = 128
A_HEADS = 16
A_KV_HEADS = 4
A_HEAD_DIM = 64
A_WINDOW = 128
B_HEADS = 8
B_Q_RANK = 512
B_KV_RANK = 256
B_NOPE = 128
B_ROPE = 64
B_V = 128
ROPE_THETA = 10000.0
A_Q_W = A_HEADS * A_HEAD_DIM
A_KV_W = A_KV_HEADS * A_HEAD_DIM
N_EXPERTS = 256
N_GROUPS = 8
GROUP_SIZE = N_EXPERTS // N_GROUPS
TOPK_GROUPS = 4
TOP_K = 8
D_EXPERT = 512
D_SHARED = 512
ROUTED_SCALE = 2.5
MOE_BLOCK = 256
DEPTH = 1
DN_ALPHA = (2 * DEPTH) ** 0.25
LN_EPS = 1e-5
RMS_EPS = 1e-6
NEG_INF = -1e30

A_SCALE = A_HEAD_DIM ** -0.5
B_SCALE = (B_NOPE + B_ROPE) ** -0.5
B_QK = B_NOPE + B_ROPE

SUBLANES = 8
LANES = 128
ROW_TILE = (D_MODEL // LANES, LANES)
assert ROW_TILE[0] == 2 * SUBLANES

VMEM_LIMIT = 56 * 1024 * 1024

NT_DIMS = (((1,), (1,)), ((), ()))


def _const_spec(shape):
    zeros = (0,) * len(shape)
    return pl.BlockSpec(shape, lambda *_: zeros, pipeline_mode=pl.Buffered(1))


def _layer_norm(z, g, b):
    mu = jnp.mean(z, axis=-1, keepdims=True)
    zc = z - mu
    var = jnp.mean(zc * zc, axis=-1, keepdims=True)
    return zc * lax.rsqrt(var + LN_EPS) * g + b


def _rms_norm(z, g):
    return z * lax.rsqrt(jnp.mean(z * z, axis=-1, keepdims=True) + RMS_EPS) * g


def _silu(z):
    return z * jax.nn.sigmoid(z)


def _embed_proj_kernel(x_ref, g_ref, b_ref, win_ref, gq_ref, wq_ref, gkv_ref, wkv_ref, cos_ref, sin_ref, rep_ref,
                       h0_ref, qa_ref, ka_ref, va_ref, qm_ref, km_ref, vm_ref):
    h = _layer_norm(x_ref[...], g_ref[...], b_ref[...])
    hb = h.astype(BF16)
    h0_ref[...] = hb
    p = jnp.dot(hb, win_ref[...], preferred_element_type=F32)
    o = 0
    qa_ref[...] = (p[:, o:o + A_Q_W] * A_SCALE).astype(BF16)
    o += A_Q_W
    ka_ref[...] = jnp.dot(p[:, o:o + A_KV_W].astype(BF16), rep_ref[...], preferred_element_type=F32).astype(BF16)
    o += A_KV_W
    va_ref[...] = jnp.dot(p[:, o:o + A_KV_W].astype(BF16), rep_ref[...], preferred_element_type=F32).astype(BF16)
    o += A_KV_W
    cq = p[:, o:o + B_Q_RANK]
    o += B_Q_RANK
    ckv = p[:, o:o + B_KV_RANK]
    o += B_KV_RANK
    kpe = p[:, o:o + B_ROPE]
    o += B_ROPE
    kpe_rot = p[:, o:o + B_ROPE]

    cos = cos_ref[...]
    sin = sin_ref[...]
    qall = jnp.dot(_rms_norm(cq, gq_ref[...]).astype(BF16), wq_ref[...], preferred_element_type=F32)
    n_nope = B_HEADS * B_NOPE
    n_pe = B_HEADS * B_ROPE
    q_pe = qall[:, n_nope:n_nope + n_pe] * cos + qall[:, n_nope + n_pe:] * sin
    kv = jnp.dot(_rms_norm(ckv, gkv_ref[...]).astype(BF16), wkv_ref[...], preferred_element_type=F32)
    k_pe = (kpe * cos[:, :B_ROPE] + kpe_rot * sin[:, :B_ROPE]).astype(BF16)
    for hd in range(B_HEADS):
        qn = (qall[:, hd * B_NOPE:(hd + 1) * B_NOPE] * B_SCALE).astype(BF16)
        qp = (q_pe[:, hd * B_ROPE:(hd + 1) * B_ROPE] * B_SCALE).astype(BF16)
        qm_ref[hd] = jnp.concatenate([qn, qp], axis=1)
        kn = kv[:, hd * B_NOPE:(hd + 1) * B_NOPE].astype(BF16)
        km_ref[hd] = jnp.concatenate([kn, k_pe], axis=1)
        vm_ref[hd] = kv[:, n_nope + hd * B_V:n_nope + (hd + 1) * B_V].astype(BF16)


def _embed_proj(x2, batch, seq, tm, ln_g, ln_b, w_in, gq, wq, gkv, wkv, rep, cos, sin):
    n = batch * seq
    nt = seq // tm
    row = lambda b, i: (b * nt + i, 0)
    head_major = lambda b, i: (b, 0, i, 0)
    in_w = w_in.shape[1]
    return pl.pallas_call(
        _embed_proj_kernel,
        grid=(batch, nt),
        in_specs=[
            pl.BlockSpec((tm, D_MODEL), row),
            _const_spec((1, D_MODEL)), _const_spec((1, D_MODEL)),
            _const_spec((D_MODEL, in_w)),
            _const_spec((1, B_Q_RANK)), _const_spec(wq.shape),
            _const_spec((1, B_KV_RANK)), _const_spec(wkv.shape),
            pl.BlockSpec((tm, B_HEADS * B_ROPE), lambda b, i: (i, 0)),
            pl.BlockSpec((tm, B_HEADS * B_ROPE), lambda b, i: (i, 0)),
            _const_spec(rep.shape),
        ],
        out_specs=[
            pl.BlockSpec((tm, D_MODEL), row),
            pl.BlockSpec((tm, A_Q_W), row),
            pl.BlockSpec((tm, A_Q_W), row),
            pl.BlockSpec((tm, A_Q_W), row),
            pl.BlockSpec((None, B_HEADS, tm, B_QK), head_major),
            pl.BlockSpec((None, B_HEADS, tm, B_QK), head_major),
            pl.BlockSpec((None, B_HEADS, tm, B_V), head_major),
        ],
        out_shape=[
            jax.ShapeDtypeStruct((n, D_MODEL), BF16),
            jax.ShapeDtypeStruct((n, A_Q_W), BF16),
            jax.ShapeDtypeStruct((n, A_Q_W), BF16),
            jax.ShapeDtypeStruct((n, A_Q_W), BF16),
            jax.ShapeDtypeStruct((batch, B_HEADS, seq, B_QK), BF16),
            jax.ShapeDtypeStruct((batch, B_HEADS, seq, B_QK), BF16),
            jax.ShapeDtypeStruct((batch, B_HEADS, seq, B_V), BF16),
        ],
        compiler_params=pltpu.CompilerParams(
            dimension_semantics=("arbitrary", "arbitrary"), vmem_limit_bytes=VMEM_LIMIT),
        name="embed_proj",
    )(x2, ln_g, ln_b, w_in, gq, wq, gkv, wkv, cos, sin, rep)


def _win_attn_kernel(sink_ref, q_ref, kp_ref, kc_ref, kn_ref, vp_ref, vc_ref, vn_ref, kmeta_ref, vmeta_ref,
                     bias_ref, o_ref):
    first = pl.program_id(1) == 0
    grp = A_HEADS // A_KV_HEADS
    gw = grp * A_HEAD_DIM
    k3 = jnp.concatenate([jnp.where(first, kmeta_ref[...], kp_ref[...]), kc_ref[...], kn_ref[...]], axis=0)
    v3 = jnp.concatenate([jnp.where(first, vmeta_ref[...], vp_ref[...]), vc_ref[...], vn_ref[...]], axis=0)
    lane_head = lax.broadcasted_iota(I32, (BLK, gw), 1) // A_HEAD_DIM
    row_head = lax.broadcasted_iota(I32, (grp * BLK, 1), 0) // BLK
    for kvh in range(A_KV_HEADS):
        cols = slice(kvh * gw, (kvh + 1) * gw)
        qg = q_ref[:, cols]
        q4 = jnp.concatenate([jnp.where(lane_head == r, qg, jnp.zeros_like(qg)) for r in range(grp)], axis=0)
        sink = jnp.zeros((grp * BLK, 1), F32)
        for r in range(grp):
            sink = jnp.where(row_head == r, sink_ref[kvh * grp + r], sink)
        s = lax.dot_general(q4, k3[:, cols], NT_DIMS, preferred_element_type=F32) + bias_ref[kvh]
        m = jnp.maximum(jnp.max(s, axis=-1, keepdims=True), sink)
        p = jnp.exp(s - m)
        denom = jnp.sum(p, axis=-1, keepdims=True) + jnp.exp(sink - m)
        ow = jnp.dot(p.astype(BF16), v3[:, cols], preferred_element_type=F32) / denom
        og = jnp.zeros((BLK, gw), F32)
        for r in range(grp):
            og = jnp.where(lane_head == r, ow[r * BLK:(r + 1) * BLK], og)
        o_ref[:, cols] = og.astype(BF16)


def _win_bias():
    qi = jnp.arange(BLK)[:, None]
    ki = jnp.arange(3 * BLK)[None, :]
    dist = jnp.abs(BLK + qi - ki)
    in_window = dist <= A_WINDOW
    valid = jnp.stack([in_window, in_window & (ki >= BLK - N_META), in_window & (ki < 2 * BLK)])
    slopes = jnp.exp2(-8.0 * jnp.arange(1, A_HEADS + 1, dtype=F32) / A_HEADS)
    bias = jnp.where(valid[:, None], -slopes[None, :, None, None] * dist.astype(F32)[None, None], NEG_INF)
    return bias.reshape(3, A_KV_HEADS, (A_HEADS // A_KV_HEADS) * BLK, 3 * BLK)


def _win_attn(sink, qa, ka, va, kmeta, vmeta, batch, seq):
    nb = seq // BLK
    n = batch * seq
    bias = _win_bias()
    cur = lambda b, i, *_: (b * nb + i, 0)
    prev = lambda b, i, *_: (b * nb + jnp.maximum(i - 1, 0), 0)
    nxt = lambda b, i, *_: (b * nb + jnp.minimum(i + 1, nb - 1), 0)
    kv_spec = lambda f: pl.BlockSpec((BLK, A_Q_W), f)
    const = lambda b, i, *_: (0, 0)
    variant = lambda b, i, *_: (jnp.where(i == 0, 1, jnp.where(i == nb - 1, 2, 0)), 0, 0, 0)
    return pl.pallas_call(
        _win_attn_kernel,
        grid_spec=pltpu.PrefetchScalarGridSpec(
            num_scalar_prefetch=1,
            grid=(batch, nb),
            in_specs=[
                pl.BlockSpec((BLK, A_Q_W), cur),
                kv_spec(prev), kv_spec(cur), kv_spec(nxt),
                kv_spec(prev), kv_spec(cur), kv_spec(nxt),
                pl.BlockSpec((BLK, A_Q_W), const), pl.BlockSpec((BLK, A_Q_W), const),
                pl.BlockSpec((None,) + bias.shape[1:], variant),
            ],
            out_specs=pl.BlockSpec((BLK, A_Q_W), cur),
        ),
        out_shape=jax.ShapeDtypeStruct((n, A_Q_W), BF16),
        compiler_params=pltpu.CompilerParams(
            dimension_semantics=("arbitrary", "arbitrary"), vmem_limit_bytes=VMEM_LIMIT),
        name="win_attn",
    )(sink, qa, ka, ka, ka, va, va, va, kmeta, vmeta, bias)


MLA_KEY_CHUNKS = 2


def _mla_attn_kernel(q_ref, k_ref, v_ref, km_ref, vm_ref, o_ref):
    q = q_ref[...]
    sm = lax.dot_general(q, km_ref[...], NT_DIMS, preferred_element_type=F32)
    m = jnp.max(sm, axis=-1, keepdims=True)
    pm = jnp.exp(sm - m)
    denom = jnp.sum(pm, axis=-1, keepdims=True)
    acc = jnp.dot(pm.astype(BF16), vm_ref[...], preferred_element_type=F32)
    chunk = k_ref.shape[0] // MLA_KEY_CHUNKS
    for c in range(MLA_KEY_CHUNKS):
        keys = slice(c * chunk, (c + 1) * chunk)
        s = lax.dot_general(q, k_ref[keys, :], NT_DIMS, preferred_element_type=F32)
        m_new = jnp.maximum(m, jnp.max(s, axis=-1, keepdims=True))
        rescale = jnp.exp(m - m_new)
        p = jnp.exp(s - m_new)
        denom = rescale * denom + jnp.sum(p, axis=-1, keepdims=True)
        acc = rescale * acc + jnp.dot(p.astype(BF16), v_ref[keys, :], preferred_element_type=F32)
        m = m_new
    o_ref[...] = (acc / denom).astype(BF16)


def _mla_attn(qm, km, vm, km_meta, vm_meta, tq):
    batch, heads, seq, _ = qm.shape
    nq = seq // tq
    return pl.pallas_call(
        _mla_attn_kernel,
        grid=(batch, heads, nq),
        in_specs=[
            pl.BlockSpec((None, None, tq, B_QK), lambda b, h, i: (b, h, i, 0)),
            pl.BlockSpec((None, None, seq, B_QK), lambda b, h, i: (b, h, 0, 0)),
            pl.BlockSpec((None, None, seq, B_V), lambda b, h, i: (b, h, 0, 0)),
            pl.BlockSpec((None, None, N_META, B_QK), lambda b, h, i: (0, h, 0, 0)),
            pl.BlockSpec((None, None, N_META, B_V), lambda b, h, i: (0, h, 0, 0)),
        ],
        out_specs=pl.BlockSpec((tq, B_V), lambda b, h, i: (b * nq + i, h)),
        out_shape=jax.ShapeDtypeStruct((batch * seq, heads * B_V), BF16),
        compiler_params=pltpu.CompilerParams(
            dimension_semantics=("arbitrary", "arbitrary", "arbitrary"), vmem_limit_bytes=VMEM_LIMIT),
        name="mla_attn",
    )(qm, km, vm, km_meta, vm_meta)


def _out_proj_kernel(oa_ref, ob_ref, h0_ref, wa_ref, wb_ref, g_ref, b_ref, wr_hi_ref, wr_lo_ref,
                     h1_ref, lg_ref):
    mix = jnp.dot(oa_ref[...], wa_ref[...], preferred_element_type=F32)
    mix = mix + jnp.dot(ob_ref[...], wb_ref[...], preferred_element_type=F32)
    h1 = _layer_norm(DN_ALPHA * h0_ref[...].astype(F32) + mix, g_ref[...], b_ref[...])
    hi = h1.astype(BF16)
    hi_f = hi.astype(F32)
    lo = (h1 - hi_f).astype(BF16)
    lg = lax.dot_general(wr_hi_ref[...], hi, NT_DIMS, preferred_element_type=F32)
    lg = lg + lax.dot_general(wr_hi_ref[...], lo, NT_DIMS, preferred_element_type=F32)
    lg = lg + lax.dot_general(wr_lo_ref[...], hi, NT_DIMS, preferred_element_type=F32)
    lg_ref[...] = lg
    h1_ref[...] = hi.reshape(h1_ref.shape)


def _out_proj(oa, ob, h0, wa, wb, g, b, wr_hi, wr_lo, tm):
    n = oa.shape[0]
    row = lambda i: (i, 0)
    return pl.pallas_call(
        _out_proj_kernel,
        grid=(n // tm,),
        in_specs=[
            pl.BlockSpec((tm, A_Q_W), row),
            pl.BlockSpec((tm, B_HEADS * B_V), row),
            pl.BlockSpec((tm, D_MODEL), row),
            _const_spec(wa.shape), _const_spec(wb.shape),
            _const_spec((1, D_MODEL)), _const_spec((1, D_MODEL)),
            _const_spec(wr_hi.shape), _const_spec(wr_lo.shape),
        ],
        out_specs=[
            pl.BlockSpec((tm,) + ROW_TILE, lambda i: (i, 0, 0)),
            pl.BlockSpec((N_EXPERTS, tm), lambda i: (0, i)),
        ],
        out_shape=[
            jax.ShapeDtypeStruct((n,) + ROW_TILE, BF16),
            jax.ShapeDtypeStruct((N_EXPERTS, n), F32),
        ],
        compiler_params=pltpu.CompilerParams(
            dimension_semantics=("arbitrary",), vmem_limit_bytes=VMEM_LIMIT),
        name="out_proj",
    )(oa, ob, h0, wa, wb, g, b, wr_hi, wr_lo)


def _first_max(cur, idx, sentinel):
    m = jnp.max(cur, axis=0, keepdims=True)
    am = jnp.min(jnp.where(cur == m, idx, sentinel), axis=0, keepdims=True)
    return m, am


def _route_kernel(lg_ref, bias_ref, eidx_ref, wts_ref, rank_ref, cnt_ref, carry_ref):
    @pl.when(pl.program_id(0) == 0)
    def _():
        carry_ref[...] = jnp.zeros_like(carry_ref)

    t = lg_ref.shape[1]
    scores = jax.nn.sigmoid(lg_ref[...])
    sel = scores + bias_ref[...]
    neg_inf = jnp.float32(-jnp.inf)

    r32 = lax.broadcasted_iota(I32, (GROUP_SIZE, t), 0)
    gs = []
    for g in range(N_GROUPS):
        blk = sel[g * GROUP_SIZE:(g + 1) * GROUP_SIZE]
        m1, i1 = _first_max(blk, r32, GROUP_SIZE)
        m2 = jnp.max(jnp.where(r32 == i1, neg_inf, blk), axis=0, keepdims=True)
        gs.append(m1 + m2)
    cur = jnp.concatenate(gs, axis=0)
    r8 = lax.broadcasted_iota(I32, (N_GROUPS, t), 0)
    keep = r8 < 0
    for _ in range(TOPK_GROUPS):
        _, ig = _first_max(cur, r8, N_GROUPS)
        hit = r8 == ig
        keep = keep | hit
        cur = jnp.where(hit, neg_inf, cur)
    cur = jnp.concatenate(
        [jnp.where(keep[g:g + 1], sel[g * GROUP_SIZE:(g + 1) * GROUP_SIZE], NEG_INF) for g in range(N_GROUPS)],
        axis=0)

    row = lax.broadcasted_iota(I32, (N_EXPERTS, t), 0)
    chosen = jnp.zeros((N_EXPERTS, t), F32)
    idxs, ws = [], []
    for _ in range(TOP_K):
        _, ik = _first_max(cur, row, N_EXPERTS)
        hit = row == ik
        ws.append(jnp.sum(jnp.where(hit, scores, 0.0), axis=0, keepdims=True))
        cur = jnp.where(hit, neg_inf, cur)
        chosen = jnp.where(hit, 1.0, chosen)
        idxs.append(ik)
    eidx = jnp.concatenate(idxs, axis=0)
    w = jnp.concatenate(ws, axis=0)
    eidx_ref[...] = eidx
    wts_ref[...] = w / jnp.sum(w, axis=0, keepdims=True) * ROUTED_SCALE

    before = lax.broadcasted_iota(I32, (t, t), 0) < lax.broadcasted_iota(I32, (t, t), 1)
    prefix = jnp.dot(chosen.astype(BF16), jnp.where(before, 1.0, 0.0).astype(BF16), preferred_element_type=F32)
    prefix = prefix + carry_ref[...]
    ranks = [jnp.sum(jnp.where(row == idxs[k], prefix, 0.0), axis=0, keepdims=True) for k in range(TOP_K)]
    rank_ref[...] = jnp.concatenate(ranks, axis=0).astype(I32)
    carry_ref[...] = carry_ref[...] + jnp.sum(chosen, axis=1, keepdims=True)
    cnt_ref[...] = carry_ref[...].astype(I32)


def _route(logits_t, bias, t):
    n = logits_t.shape[1]
    col = lambda i: (0, i)
    return pl.pallas_call(
        _route_kernel,
        grid=(n // t,),
        in_specs=[pl.BlockSpec((N_EXPERTS, t), col), _const_spec((N_EXPERTS, 1))],
        out_specs=[
            pl.BlockSpec((TOP_K, t), col), pl.BlockSpec((TOP_K, t), col), pl.BlockSpec((TOP_K, t), col),
            pl.BlockSpec((N_EXPERTS, 1), lambda i: (0, 0)),
        ],
        out_shape=[
            jax.ShapeDtypeStruct((TOP_K, n), I32),
            jax.ShapeDtypeStruct((TOP_K, n), F32),
            jax.ShapeDtypeStruct((TOP_K, n), I32),
            jax.ShapeDtypeStruct((N_EXPERTS, 1), I32),
        ],
        scratch_shapes=[pltpu.VMEM((N_EXPERTS, 1), F32)],
        compiler_params=pltpu.CompilerParams(
            dimension_semantics=("arbitrary",), vmem_limit_bytes=VMEM_LIMIT),
        name="route",
    )(logits_t, bias)


def _positions_kernel(eidx_ref, rank_ref, start_ref, pos_ref):
    t = eidx_ref.shape[1]
    row = lax.broadcasted_iota(I32, (N_EXPERTS, t), 0)
    start = start_ref[...]
    eidx = eidx_ref[...]
    base = [jnp.sum(jnp.where(row == eidx[k:k + 1], start, 0), axis=0, keepdims=True) for k in range(TOP_K)]
    pos_ref[...] = jnp.concatenate(base, axis=0) + rank_ref[...]


def _positions(eidx, rank, starts, t):
    n = eidx.shape[1]
    col = lambda i: (0, i)
    return pl.pallas_call(
        _positions_kernel,
        grid=(n // t,),
        in_specs=[pl.BlockSpec((TOP_K, t), col), pl.BlockSpec((TOP_K, t), col), _const_spec((N_EXPERTS, 1))],
        out_specs=pl.BlockSpec((None, TOP_K, t), lambda i: (i, 0, 0)),
        out_shape=jax.ShapeDtypeStruct((n // t, TOP_K, t), I32),
        compiler_params=pltpu.CompilerParams(dimension_semantics=("arbitrary",)),
        name="positions",
    )(eidx, rank, starts)


def _zero_fill(row0_ref, len_ref, nu_ref, zbuf, xs_ref, sem, wait):
    def run(copy):
        copy.wait() if wait else copy.start()

    def per_expert(e, carry):
        r0 = row0_ref[e]
        ln = len_ref[e]
        bit = MOE_BLOCK // 2
        while bit >= 1:
            off = ln - (ln & (2 * bit - 1))

            @pl.when((ln & bit) != 0)
            def _(bit=bit, off=off):
                run(pltpu.make_async_copy(zbuf.at[pl.ds(0, bit)], xs_ref.at[pl.ds(r0 + off, bit)], sem))
            bit //= 2
        return carry

    lax.fori_loop(0, N_EXPERTS, per_expert, 0)

    def per_block(g, carry):
        run(pltpu.make_async_copy(zbuf, xs_ref.at[pl.ds(g * MOE_BLOCK, MOE_BLOCK)], sem))
        return carry

    lax.fori_loop(nu_ref[0], xs_ref.shape[0] // MOE_BLOCK, per_block, 0)


def _dispatch_kernel(row0_ref, len_ref, nu_ref, pos_ref, h_ref, xs_ref, zbuf, sem, zsem):
    td = h_ref.shape[0]

    @pl.when(pl.program_id(0) == 0)
    def _():
        zbuf[...] = jnp.zeros_like(zbuf)
        _zero_fill(row0_ref, len_ref, nu_ref, zbuf, xs_ref, zsem, wait=False)

    def issue(t, carry):
        for k in range(TOP_K):
            pltpu.make_async_copy(h_ref.at[t], xs_ref.at[pos_ref[k, t]], sem).start(priority=k % 2)
        return carry

    lax.fori_loop(0, td, issue, 0, unroll=4)
    for _ in range(TOP_K):
        pltpu.make_async_copy(h_ref, xs_ref.at[pl.ds(0, td)], sem).wait()

    @pl.when(pl.program_id(0) == pl.num_programs(0) - 1)
    def _():
        _zero_fill(row0_ref, len_ref, nu_ref, zbuf, xs_ref, zsem, wait=True)


def _dispatch(pad_row0, pad_len, n_used, pos3, h1, n_rows, td):
    n = h1.shape[0]
    tp = pos3.shape[2]
    per = tp // td
    return pl.pallas_call(
        _dispatch_kernel,
        grid_spec=pltpu.PrefetchScalarGridSpec(
            num_scalar_prefetch=3,
            grid=(n // td,),
            in_specs=[
                pl.BlockSpec((None, TOP_K, td), lambda i, *_: (i // per, 0, i % per), memory_space=pltpu.SMEM),
                pl.BlockSpec((td,) + ROW_TILE, lambda i, *_: (i, 0, 0)),
            ],
            out_specs=pl.BlockSpec(memory_space=pl.ANY),
            scratch_shapes=[pltpu.VMEM((MOE_BLOCK,) + ROW_TILE, BF16), pltpu.SemaphoreType.DMA,
                            pltpu.SemaphoreType.DMA],
        ),
        out_shape=jax.ShapeDtypeStruct((n_rows,) + ROW_TILE, BF16),
        compiler_params=pltpu.CompilerParams(dimension_semantics=("arbitrary",)),
        name="dispatch",
    )(pad_row0, pad_len, n_used, pos3, h1)


CAST_CHUNK_ELEMS = 32 * 1024


def _cast_rows(src, dst):
    chunk = CAST_CHUNK_ELEMS // src.shape[1]
    for r in range(0, src.shape[0], chunk):
        dst[r:r + chunk, :] = src[r:r + chunk, :].astype(dst.dtype)


WEIGHT_DMA_PRIORITY = 1


def _moe_kernel(be_ref, nv_ref, nu_ref, nxt_ref, slot_ref, x_ref, wg_hbm, wu_hbm, wd_hbm, y_ref,
                wg_land, wu_land, wd_land, wg_s, wu_s, wd_s, sems):
    g = pl.program_id(0)
    nv = nv_ref[g]
    expert = be_ref[g]
    slot = slot_ref[g]
    new_expert = (g == 0) | (expert != be_ref[jnp.maximum(g - 1, 0)])

    def weight_copies(e, s):
        return (pltpu.make_async_copy(wg_hbm.at[e], wg_land.at[s], sems.at[s, 0]),
                pltpu.make_async_copy(wu_hbm.at[e], wu_land.at[s], sems.at[s, 1]),
                pltpu.make_async_copy(wd_hbm.at[e], wd_land.at[s], sems.at[s, 2]))

    @pl.when(g == 0)
    def _():
        for c in weight_copies(expert, slot):
            c.start(priority=WEIGHT_DMA_PRIORITY)

    def expert_mlp():
        x = x_ref[...].reshape(MOE_BLOCK, D_MODEL)
        gate = jnp.dot(x, wg_s[...], preferred_element_type=F32)
        up = jnp.dot(x, wu_s[...], preferred_element_type=F32)
        act = (_silu(gate) * up).astype(BF16)
        y = jnp.dot(act, wd_s[...], preferred_element_type=F32)
        y_ref[...] = y.astype(BF16).reshape(y_ref.shape)

    @pl.when((nv > 0) & new_expert)
    def _():
        nxt = nxt_ref[g]

        @pl.when(nxt >= 0)
        def _():
            for c in weight_copies(nxt, 1 - slot):
                c.start(priority=WEIGHT_DMA_PRIORITY)

        for c in weight_copies(expert, slot):
            c.wait()
        _cast_rows(wg_land.at[slot], wg_s)
        _cast_rows(wu_land.at[slot], wu_s)
        _cast_rows(wd_land.at[slot], wd_s)
        expert_mlp()

    @pl.when((nv > 0) & jnp.logical_not(new_expert))
    def _():
        expert_mlp()

    @pl.when(nv <= 0)
    def _():
        y_ref[...] = jnp.zeros_like(y_ref)


def _moe(block_e, n_valid, n_used, next_e, slot, xs, w_gate, w_up, w_down):
    n_rows = xs.shape[0]
    nblk = n_rows // MOE_BLOCK
    used = lambda g, be, nv, nu, nx, sl: (jnp.minimum(g, nu[0] - 1), 0, 0)
    yout = lambda g, be, nv, nu, nx, sl: (g, 0, 0)
    hbm = pl.BlockSpec(memory_space=pl.ANY)
    return pl.pallas_call(
        _moe_kernel,
        grid_spec=pltpu.PrefetchScalarGridSpec(
            num_scalar_prefetch=5,
            grid=(nblk,),
            in_specs=[pl.BlockSpec((MOE_BLOCK,) + ROW_TILE, used), hbm, hbm, hbm],
            out_specs=pl.BlockSpec((MOE_BLOCK,) + ROW_TILE, yout),
            scratch_shapes=[
                pltpu.VMEM((2, D_MODEL, D_EXPERT), F32),
                pltpu.VMEM((2, D_MODEL, D_EXPERT), F32),
                pltpu.VMEM((2, D_EXPERT, D_MODEL), F32),
                pltpu.VMEM((D_MODEL, D_EXPERT), BF16),
                pltpu.VMEM((D_MODEL, D_EXPERT), BF16),
                pltpu.VMEM((D_EXPERT, D_MODEL), BF16),
                pltpu.SemaphoreType.DMA((2, 3)),
            ],
        ),
        out_shape=jax.ShapeDtypeStruct(xs.shape, xs.dtype),
        compiler_params=pltpu.CompilerParams(
            dimension_semantics=("arbitrary",), vmem_limit_bytes=VMEM_LIMIT),
        name="moe",
    )(block_e, n_valid, n_used, next_e, slot, xs, w_gate, w_up, w_down)


def _combine_kernel(pos_ref, nxt_ref, h_ref, wts_ref, ys_ref, wsg_ref, wsu_ref, wsd_ref, g_ref, b_ref,
                    o_ref, gbuf_a, gbuf_b, acc_ref, sem_a, sem_b):
    tc = h_ref.shape[0] // 2
    step = pl.program_id(0)

    def issue(pos, t0, gbuf, sem):
        for t in range(tc):
            for k in range(TOP_K):
                pltpu.make_async_copy(ys_ref.at[pos[k, t0 + t]], gbuf.at[k, t], sem).start(priority=k % 2)

    def drain(gbuf, sem):
        for k in range(TOP_K):
            pltpu.make_async_copy(ys_ref.at[pl.ds(0, tc)], gbuf.at[k], sem).wait()

    def shared_expert(rows):
        hb = h_ref[rows].reshape(tc, D_MODEL)
        act = _silu(jnp.dot(hb, wsg_ref[...], preferred_element_type=F32))
        act = (act * jnp.dot(hb, wsu_ref[...], preferred_element_type=F32)).astype(BF16)
        return DN_ALPHA * hb.astype(F32) + jnp.dot(act, wsd_ref[...], preferred_element_type=F32)

    def finish(rows, base, gbuf):
        for r in range(0, tc, SUBLANES):
            wts = wts_ref[rows.start + r:rows.start + r + SUBLANES]
            acc = jnp.zeros((SUBLANES,) + ROW_TILE, F32)
            for k in range(TOP_K):
                wk = jnp.broadcast_to(wts[:, k:k + 1, :], acc.shape)
                acc = acc + wk * gbuf[k, r:r + SUBLANES].astype(F32)
            acc_ref[r:r + SUBLANES, :] = acc.reshape(SUBLANES, D_MODEL)
        o_ref[rows, :] = _layer_norm(base + acc_ref[...], g_ref[...], b_ref[...])

    first, second = slice(0, tc), slice(tc, 2 * tc)

    @pl.when(step == 0)
    def _():
        issue(pos_ref, 0, gbuf_a, sem_a)

    drain(gbuf_a, sem_a)
    finish(first, shared_expert(first), gbuf_a)
    issue(pos_ref, tc, gbuf_b, sem_b)

    drain(gbuf_b, sem_b)
    finish(second, shared_expert(second), gbuf_b)
    issue(nxt_ref, 0, gbuf_a, sem_a)

    @pl.when(step == pl.num_programs(0) - 1)
    def _():
        drain(gbuf_a, sem_a)


def _combine(pos3, h1, wts3, ys, wsg, wsu, wsd, g, b, tc):
    n = h1.shape[0]
    tp = pos3.shape[2]
    per = tp // tc
    steps = n // (2 * tc)
    row3 = lambda i: (i, 0, 0)

    def next_tile(i):
        t = jnp.minimum(2 * i + 2, 2 * steps - 1)
        return (t // per, 0, t % per)

    return pl.pallas_call(
        _combine_kernel,
        grid=(steps,),
        in_specs=[
            pl.BlockSpec((None, TOP_K, 2 * tc), lambda i: (2 * i // per, 0, i % (per // 2)), memory_space=pltpu.SMEM),
            pl.BlockSpec((None, TOP_K, tc), next_tile, memory_space=pltpu.SMEM),
            pl.BlockSpec((2 * tc,) + ROW_TILE, row3),
            pl.BlockSpec((2 * tc,) + wts3.shape[1:], row3),
            pl.BlockSpec(memory_space=pl.ANY),
            _const_spec(wsg.shape), _const_spec(wsu.shape), _const_spec(wsd.shape),
            _const_spec((1, D_MODEL)), _const_spec((1, D_MODEL)),
        ],
        out_specs=pl.BlockSpec((2 * tc, D_MODEL), lambda i: (i, 0)),
        out_shape=jax.ShapeDtypeStruct((n, D_MODEL), F32),
        scratch_shapes=[pltpu.VMEM((TOP_K, tc) + ROW_TILE, BF16), pltpu.VMEM((TOP_K, tc) + ROW_TILE, BF16),
                        pltpu.VMEM((tc, D_MODEL), F32), pltpu.SemaphoreType.DMA, pltpu.SemaphoreType.DMA],
        compiler_params=pltpu.CompilerParams(
            dimension_semantics=("arbitrary",), vmem_limit_bytes=VMEM_LIMIT),
        name="combine",
    )(pos3, pos3, h1, wts3, ys, wsg, wsu, wsd, g, b)


def _rotate_half_cols(w):
    half = w.shape[-1] // 2
    return jnp.concatenate([-w[..., half:], w[..., :half]], axis=-1)


def _rope_tables(pos):
    inv_freq = ROPE_THETA ** (-jnp.arange(0, B_ROPE, 2, dtype=F32) / B_ROPE)
    ang = pos[:, None] * inv_freq[None, :]
    cos = jnp.tile(jnp.cos(ang), (1, 2 * B_HEADS))
    sin = jnp.tile(jnp.sin(ang), (1, 2 * B_HEADS))
    return cos, sin


def kernel(x, meta_tokens, ln_emb_g, ln_emb_b, w_in, q_norm_g, w_q_up, kv_norm_g, w_kv_up, sink_logits, w_out,
           ln_mix_g, ln_mix_b, w_router, router_bias, w_exp_gate, w_exp_up, w_exp_down, w_sh_gate, w_sh_up,
           w_sh_down, ln_ffn_g, ln_ffn_b):
    batch, seq, d = x.shape
    assert d == D_MODEL and seq % 512 == 0 and w_in.shape[0] == DEPTH
    n = batch * seq
    row2 = lambda v: v.reshape(1, -1).astype(F32)

    w_in0 = w_in[0]
    kpe_cols = w_in0[:, -B_ROPE:]
    w_in_b = jnp.concatenate([w_in0, _rotate_half_cols(kpe_cols)], axis=1).astype(BF16)
    wq = w_q_up[0]
    wq_pe = wq[:, :, B_NOPE:]
    wq_b = jnp.concatenate([
        wq[:, :, :B_NOPE].reshape(B_Q_RANK, -1),
        wq_pe.reshape(B_Q_RANK, -1),
        _rotate_half_cols(wq_pe).reshape(B_Q_RANK, -1)], axis=1).astype(BF16)
    wkv = w_kv_up[0]
    wkv_b = jnp.concatenate([
        wkv[:, :, :B_NOPE].reshape(B_KV_RANK, -1), wkv[:, :, B_NOPE:].reshape(B_KV_RANK, -1)], axis=1).astype(BF16)
    w_out_b = w_out[0].astype(BF16)
    wr_t = w_router[0].astype(F32).T
    wr_hi = wr_t.astype(BF16)
    wr_lo = (wr_t - wr_hi.astype(F32)).astype(BF16)

    cos_m, sin_m = _rope_tables(jnp.arange(N_META, dtype=F32))
    cos_x, sin_x = _rope_tables(jnp.arange(N_META, N_META + seq, dtype=F32))
    rep_col = jnp.arange(A_Q_W)
    rep_src = rep_col // (A_Q_W // A_KV_HEADS) * A_HEAD_DIM + rep_col % A_HEAD_DIM
    rep = (jnp.arange(A_KV_W)[:, None] == rep_src[None, :]).astype(BF16)
    proj_args = (row2(ln_emb_g), row2(ln_emb_b), w_in_b, row2(q_norm_g[0]), wq_b, row2(kv_norm_g[0]), wkv_b, rep)
    _, _, ka_m, va_m, _, km_m, vm_m = _embed_proj(
        meta_tokens.astype(F32), 1, N_META, N_META, *proj_args, cos_m, sin_m)
    h0, qa, ka, va, qm, km, vm = _embed_proj(x.reshape(n, d), batch, seq, 256, *proj_args, cos_x, sin_x)

    pad = jnp.zeros((BLK - N_META, A_Q_W), BF16)
    oa = _win_attn(sink_logits[0].astype(F32), qa, ka, va,
                   jnp.concatenate([pad, ka_m], axis=0), jnp.concatenate([pad, va_m], axis=0), batch, seq)
    ob = _mla_attn(qm, km, vm, km_m, vm_m, min(seq, 1024))

    h1, logits_t = _out_proj(oa, ob, h0, w_out_b[:A_Q_W], w_out_b[A_Q_W:], row2(ln_mix_g[0]), row2(ln_mix_b[0]),
                              wr_hi, wr_lo, 512)

    eidx, wts, rank, counts = _route(logits_t, router_bias[0].astype(F32).reshape(N_EXPERTS, 1), 512)

    counts = counts[:, 0]
    nk = n * TOP_K
    padded = (counts + MOE_BLOCK - 1) // MOE_BLOCK * MOE_BLOCK
    pad_ends = jnp.cumsum(padded)
    pad_starts = pad_ends - padded
    n_rows = (nk + MOE_BLOCK - 1) // MOE_BLOCK * MOE_BLOCK + N_EXPERTS * MOE_BLOCK
    n_blocks = n_rows // MOE_BLOCK
    blk_row0 = jnp.arange(n_blocks, dtype=I32) * MOE_BLOCK
    block_e = jnp.minimum(jnp.sum(pad_ends[None, :] <= blk_row0[:, None], axis=1), N_EXPERTS - 1).astype(I32)
    owner = block_e[:, None] == jnp.arange(N_EXPERTS, dtype=I32)[None, :]
    per_block = lambda table: jnp.sum(jnp.where(owner, table[None, :], 0), axis=1)
    seg_ends = (pad_starts + counts).astype(I32)
    n_valid = jnp.clip(per_block(seg_ends) - blk_row0, 0, MOE_BLOCK).astype(I32)
    n_used = (pad_ends[-1:] // MOE_BLOCK).astype(I32)
    eid = jnp.arange(N_EXPERTS, dtype=I32)
    later_used = jnp.where((eid[None, :] > eid[:, None]) & (counts[None, :] > 0), eid[None, :], N_EXPERTS)
    next_used = jnp.min(later_used, axis=1)
    next_e = per_block(jnp.where(next_used < N_EXPERTS, next_used, -1).astype(I32)).astype(I32)
    land_slot = (per_block(jnp.cumsum((counts > 0).astype(I32)).astype(I32)) % 2).astype(I32)

    pos3 = _positions(eidx, rank, pad_starts.astype(I32).reshape(N_EXPERTS, 1), 512)
    xs = _dispatch(seg_ends, (padded - counts).astype(I32), n_used, pos3, h1, n_rows, 256)
    ys = _moe(block_e, n_valid, n_used, next_e, land_slot, xs, w_exp_gate.reshape(w_exp_gate.shape[1:]),
              w_exp_up.reshape(w_exp_up.shape[1:]), w_exp_down.reshape(w_exp_down.shape[1:]))
    wts3 = jnp.broadcast_to(wts.T[:, :, None], (n, TOP_K, BLK))
    out = _combine(pos3, h1, wts3, ys, w_sh_gate[0].astype(BF16), w_sh_up[0].astype(BF16),
                   w_sh_down[0].astype(BF16), row2(ln_ffn_g[0]), row2(ln_ffn_b[0]), 128)
    return out.reshape(batch, seq, d)
```

```python
import functools

import jax
import jax.numpy as jnp
from jax import lax
from jax.experimental import pallas as pl
from jax.experimental.pallas import tpu as pltpu

F32 = jnp.float32
BF16 = jnp.bfloat16
I32 = jnp.int32

D_MODEL = 2048
N_META = 16
BLK = 128
A_HEADS = 16
A_KV_HEADS = 4
A_HEAD_DIM = 64
A_WINDOW = 128
B_HEADS = 8
B_Q_RANK = 512
B_KV_RANK = 256
B_NOPE = 128
B_ROPE = 64
B_V = 128
ROPE_THETA = 10000.0
A_Q_W = A_HEADS * A_HEAD_DIM
A_KV_W = A_KV_HEADS * A_HEAD_DIM
N_EXPERTS = 256
N_GROUPS = 8
GROUP_SIZE = N_EXPERTS // N_GROUPS
TOPK_GROUPS = 4
TOP_K = 8
D_EXPERT = 512
D_SHARED = 512
ROUTED_SCALE = 2.5
MOE_BLOCK = 256
DEPTH = 1
DN_ALPHA = (2 * DEPTH) ** 0.25
LN_EPS = 1e-5
RMS_EPS = 1e-6
NEG_INF = -1e30

A_SCALE = A_HEAD_DIM ** -0.5
B_SCALE = (B_NOPE + B_ROPE) ** -0.5
B_QK = B_NOPE + B_ROPE

SUBLANES = 8
LANES = 128
ROW_TILE = (D_MODEL // LANES, LANES)
assert ROW_TILE[0] == 2 * SUBLANES

VMEM_LIMIT = 56 * 1024 * 1024

NT_DIMS = (((1,), (1,)), ((), ()))


def _const_spec(shape):
    zeros = (0,) * len(shape)
    return pl.BlockSpec(shape, lambda *_: zeros, pipeline_mode=pl.Buffered(1))


def _layer_norm(z, g, b):
    mu = jnp.mean(z, axis=-1, keepdims=True)
    zc = z - mu
    var = jnp.mean(zc * zc, axis=-1, keepdims=True)
    return zc * lax.rsqrt(var + LN_EPS) * g + b


def _rms_norm(z, g):
    return z * lax.rsqrt(jnp.mean(z * z, axis=-1, keepdims=True) + RMS_EPS) * g


def _silu(z):
    return z * jax.nn.sigmoid(z)


def _embed_proj_kernel(x_ref, g_ref, b_ref, win_ref, gq_ref, wq_ref, gkv_ref, wkv_ref, cos_ref, sin_ref, rep_ref,
                       h0_ref, qa_ref, ka_ref, va_ref, qm_ref, km_ref, vm_ref):
    h = _layer_norm(x_ref[...], g_ref[...], b_ref[...])
    hb = h.astype(BF16)
    h0_ref[...] = hb
    p = jnp.dot(hb, win_ref[...], preferred_element_type=F32)
    o = 0
    qa_ref[...] = (p[:, o:o + A_Q_W] * A_SCALE).astype(BF16)
    o += A_Q_W
    ka_ref[...] = jnp.dot(p[:, o:o + A_KV_W].astype(BF16), rep_ref[...], preferred_element_type=F32).astype(BF16)
    o += A_KV_W
    va_ref[...] = jnp.dot(p[:, o:o + A_KV_W].astype(BF16), rep_ref[...], preferred_element_type=F32).astype(BF16)
    o += A_KV_W
    cq = p[:, o:o + B_Q_RANK]
    o += B_Q_RANK
    ckv = p[:, o:o + B_KV_RANK]
    o += B_KV_RANK
    kpe = p[:, o:o + B_ROPE]
    o += B_ROPE
    kpe_rot = p[:, o:o + B_ROPE]

    cos = cos_ref[...]
    sin = sin_ref[...]
    qall = jnp.dot(_rms_norm(cq, gq_ref[...]).astype(BF16), wq_ref[...], preferred_element_type=F32)
    n_nope = B_HEADS * B_NOPE
    n_pe = B_HEADS * B_ROPE
    q_pe = qall[:, n_nope:n_nope + n_pe] * cos + qall[:, n_nope + n_pe:] * sin
    kv = jnp.dot(_rms_norm(ckv, gkv_ref[...]).astype(BF16), wkv_ref[...], preferred_element_type=F32)
    k_pe = (kpe * cos[:, :B_ROPE] + kpe_rot * sin[:, :B_ROPE]).astype(BF16)
    for hd in range(B_HEADS):
        qn = (qall[:, hd * B_NOPE:(hd + 1) * B_NOPE] * B_SCALE).astype(BF16)
        qp = (q_pe[:, hd * B_ROPE:(hd + 1) * B_ROPE] * B_SCALE).astype(BF16)
        qm_ref[hd] = jnp.concatenate([qn, qp], axis=1)
        kn = kv[:, hd * B_NOPE:(hd + 1) * B_NOPE].astype(BF16)
        km_ref[hd] = jnp.concatenate([kn, k_pe], axis=1)
        vm_ref[hd] = kv[:, n_nope + hd * B_V:n_nope + (hd + 1) * B_V].astype(BF16)


def _embed_proj(x2, batch, seq, tm, ln_g, ln_b, w_in, gq, wq, gkv, wkv, rep, cos, sin):
    n = batch * seq
    nt = seq // tm
    row = lambda b, i: (b * nt + i, 0)
    head_major = lambda b, i: (b, 0, i, 0)
    in_w = w_in.shape[1]
    return pl.pallas_call(
        _embed_proj_kernel,
        grid=(batch, nt),
        in_specs=[
            pl.BlockSpec((tm, D_MODEL), row),
            _const_spec((1, D_MODEL)), _const_spec((1, D_MODEL)),
            _const_spec((D_MODEL, in_w)),
            _const_spec((1, B_Q_RANK)), _const_spec(wq.shape),
            _const_spec((1, B_KV_RANK)), _const_spec(wkv.shape),
            pl.BlockSpec((tm, B_HEADS * B_ROPE), lambda b, i: (i, 0)),
            pl.BlockSpec((tm, B_HEADS * B_ROPE), lambda b, i: (i, 0)),
            _const_spec(rep.shape),
        ],
        out_specs=[
            pl.BlockSpec((tm, D_MODEL), row),
            pl.BlockSpec((tm, A_Q_W), row),
            pl.BlockSpec((tm, A_Q_W), row),
            pl.BlockSpec((tm, A_Q_W), row),
            pl.BlockSpec((None, B_HEADS, tm, B_QK), head_major),
            pl.BlockSpec((None, B_HEADS, tm, B_QK), head_major),
            pl.BlockSpec((None, B_HEADS, tm, B_V), head_major),
        ],
        out_shape=[
            jax.ShapeDtypeStruct((n, D_MODEL), BF16),
            jax.ShapeDtypeStruct((n, A_Q_W), BF16),
            jax.ShapeDtypeStruct((n, A_Q_W), BF16),
            jax.ShapeDtypeStruct((n, A_Q_W), BF16),
            jax.ShapeDtypeStruct((batch, B_HEADS, seq, B_QK), BF16),
            jax.ShapeDtypeStruct((batch, B_HEADS, seq, B_QK), BF16),
            jax.ShapeDtypeStruct((batch, B_HEADS, seq, B_V), BF16),
        ],
        compiler_params=pltpu.CompilerParams(
            dimension_semantics=("arbitrary", "arbitrary"), vmem_limit_bytes=VMEM_LIMIT),
        name="embed_proj",
    )(x2, ln_g, ln_b, w_in, gq, wq, gkv, wkv, cos, sin, rep)


def _win_attn_kernel(sink_ref, q_ref, kp_ref, kc_ref, kn_ref, vp_ref, vc_ref, vn_ref, kmeta_ref, vmeta_ref,
                     bias_ref, o_ref):
    first = pl.program_id(1) == 0
    grp = A_HEADS // A_KV_HEADS
    gw = grp * A_HEAD_DIM
    k3 = jnp.concatenate([jnp.where(first, kmeta_ref[...], kp_ref[...]), kc_ref[...], kn_ref[...]], axis=0)
    v3 = jnp.concatenate([jnp.where(first, vmeta_ref[...], vp_ref[...]), vc_ref[...], vn_ref[...]], axis=0)
    lane_head = lax.broadcasted_iota(I32, (BLK, gw), 1) // A_HEAD_DIM
    row_head = lax.broadcasted_iota(I32, (grp * BLK, 1), 0) // BLK
    for kvh in range(A_KV_HEADS):
        cols = slice(kvh * gw, (kvh + 1) * gw)
        qg = q_ref[:, cols]
        q4 = jnp.concatenate([jnp.where(lane_head == r, qg, jnp.zeros_like(qg)) for r in range(grp)], axis=0)
        sink = jnp.zeros((grp * BLK, 1), F32)
        for r in range(grp):
            sink = jnp.where(row_head == r, sink_ref[kvh * grp + r], sink)
        s = lax.dot_general(q4, k3[:, cols], NT_DIMS, preferred_element_type=F32) + bias_ref[kvh]
        m = jnp.maximum(jnp.max(s, axis=-1, keepdims=True), sink)
        p = jnp.exp(s - m)
        denom = jnp.sum(p, axis=-1, keepdims=True) + jnp.exp(sink - m)
        ow = jnp.dot(p.astype(BF16), v3[:, cols], preferred_element_type=F32) / denom
        og = jnp.zeros((BLK, gw), F32)
        for r in range(grp):
            og = jnp.where(lane_head == r, ow[r * BLK:(r + 1) * BLK], og)
        o_ref[:, cols] = og.astype(BF16)


def _win_bias():
    qi = jnp.arange(BLK)[:, None]
    ki = jnp.arange(3 * BLK)[None, :]
    dist = jnp.abs(BLK + qi - ki)
    in_window = dist <= A_WINDOW
    valid = jnp.stack([in_window, in_window & (ki >= BLK - N_META), in_window & (ki < 2 * BLK)])
    slopes = jnp.exp2(-8.0 * jnp.arange(1, A_HEADS + 1, dtype=F32) / A_HEADS)
    bias = jnp.where(valid[:, None], -slopes[None, :, None, None] * dist.astype(F32)[None, None], NEG_INF)
    return bias.reshape(3, A_KV_HEADS, (A_HEADS // A_KV_HEADS) * BLK, 3 * BLK)


def _win_attn(sink, qa, ka, va, kmeta, vmeta, batch, seq):
    nb = seq // BLK
    n = batch * seq
    bias = _win_bias()
    cur = lambda b, i, *_: (b * nb + i, 0)
    prev = lambda b, i, *_: (b * nb + jnp.maximum(i - 1, 0), 0)
    nxt = lambda b, i, *_: (b * nb + jnp.minimum(i + 1, nb - 1), 0)
    kv_spec = lambda f: pl.BlockSpec((BLK, A_Q_W), f)
    const = lambda b, i, *_: (0, 0)
    variant = lambda b, i, *_: (jnp.where(i == 0, 1, jnp.where(i == nb - 1, 2, 0)), 0, 0, 0)
    return pl.pallas_call(
        _win_attn_kernel,
        grid_spec=pltpu.PrefetchScalarGridSpec(
            num_scalar_prefetch=1,
            grid=(batch, nb),
            in_specs=[
                pl.BlockSpec((BLK, A_Q_W), cur),
                kv_spec(prev), kv_spec(cur), kv_spec(nxt),
                kv_spec(prev), kv_spec(cur), kv_spec(nxt),
                pl.BlockSpec((BLK, A_Q_W), const), pl.BlockSpec((BLK, A_Q_W), const),
                pl.BlockSpec((None,) + bias.shape[1:], variant),
            ],
            out_specs=pl.BlockSpec((BLK, A_Q_W), cur),
        ),
        out_shape=jax.ShapeDtypeStruct((n, A_Q_W), BF16),
        compiler_params=pltpu.CompilerParams(
            dimension_semantics=("arbitrary", "arbitrary"), vmem_limit_bytes=VMEM_LIMIT),
        name="win_attn",
    )(sink, qa, ka, ka, ka, va, va, va, kmeta, vmeta, bias)


MLA_KEY_CHUNKS = 2


def _mla_attn_kernel(q_ref, k_ref, v_ref, km_ref, vm_ref, o_ref):
    q = q_ref[...]
    sm = lax.dot_general(q, km_ref[...], NT_DIMS, preferred_element_type=F32)
    m = jnp.max(sm, axis=-1, keepdims=True)
    pm = jnp.exp(sm - m)
    denom = jnp.sum(pm, axis=-1, keepdims=True)
    acc = jnp.dot(pm.astype(BF16), vm_ref[...], preferred_element_type=F32)
    chunk = k_ref.shape[0] // MLA_KEY_CHUNKS
    for c in range(MLA_KEY_CHUNKS):
        keys = slice(c * chunk, (c + 1) * chunk)
        s = lax.dot_general(q, k_ref[keys, :], NT_DIMS, preferred_element_type=F32)
        m_new = jnp.maximum(m, jnp.max(s, axis=-1, keepdims=True))
        rescale = jnp.exp(m - m_new)
        p = jnp.exp(s - m_new)
        denom = rescale * denom + jnp.sum(p, axis=-1, keepdims=True)
        acc = rescale * acc + jnp.dot(p.astype(BF16), v_ref[keys, :], preferred_element_type=F32)
        m = m_new
    o_ref[...] = (acc / denom).astype(BF16)


def _mla_attn(qm, km, vm, km_meta, vm_meta, tq):
    batch, heads, seq, _ = qm.shape
    nq = seq // tq
    return pl.pallas_call(
        _mla_attn_kernel,
        grid=(batch, heads, nq),
        in_specs=[
            pl.BlockSpec((None, None, tq, B_QK), lambda b, h, i: (b, h, i, 0)),
            pl.BlockSpec((None, None, seq, B_QK), lambda b, h, i: (b, h, 0, 0)),
            pl.BlockSpec((None, None, seq, B_V), lambda b, h, i: (b, h, 0, 0)),
            pl.BlockSpec((None, None, N_META, B_QK), lambda b, h, i: (0, h, 0, 0)),
            pl.BlockSpec((None, None, N_META, B_V), lambda b, h, i: (0, h, 0, 0)),
        ],
        out_specs=pl.BlockSpec((tq, B_V), lambda b, h, i: (b * nq + i, h)),
        out_shape=jax.ShapeDtypeStruct((batch * seq, heads * B_V), BF16),
        compiler_params=pltpu.CompilerParams(
            dimension_semantics=("arbitrary", "arbitrary", "arbitrary"), vmem_limit_bytes=VMEM_LIMIT),
        name="mla_attn",
    )(qm, km, vm, km_meta, vm_meta)


def _out_proj_kernel(oa_ref, ob_ref, h0_ref, wa_ref, wb_ref, g_ref, b_ref, wr_hi_ref, wr_lo_ref,
                     h1_ref, lg_ref):
    mix = jnp.dot(oa_ref[...], wa_ref[...], preferred_element_type=F32)
    mix = mix + jnp.dot(ob_ref[...], wb_ref[...], preferred_element_type=F32)
    h1 = _layer_norm(DN_ALPHA * h0_ref[...].astype(F32) + mix, g_ref[...], b_ref[...])
    hi = h1.astype(BF16)
    hi_f = hi.astype(F32)
    lo = (h1 - hi_f).astype(BF16)
    lg = lax.dot_general(wr_hi_ref[...], hi, NT_DIMS, preferred_element_type=F32)
    lg = lg + lax.dot_general(wr_hi_ref[...], lo, NT_DIMS, preferred_element_type=F32)
    lg = lg + lax.dot_general(wr_lo_ref[...], hi, NT_DIMS, preferred_element_type=F32)
    lg_ref[...] = lg
    h1_ref[...] = hi.reshape(h1_ref.shape)


def _out_proj(oa, ob, h0, wa, wb, g, b, wr_hi, wr_lo, tm):
    n = oa.shape[0]
    row = lambda i: (i, 0)
    return pl.pallas_call(
        _out_proj_kernel,
        grid=(n // tm,),
        in_specs=[
            pl.BlockSpec((tm, A_Q_W), row),
            pl.BlockSpec((tm, B_HEADS * B_V), row),
            pl.BlockSpec((tm, D_MODEL), row),
            _const_spec(wa.shape), _const_spec(wb.shape),
            _const_spec((1, D_MODEL)), _const_spec((1, D_MODEL)),
            _const_spec(wr_hi.shape), _const_spec(wr_lo.shape),
        ],
        out_specs=[
            pl.BlockSpec((tm,) + ROW_TILE, lambda i: (i, 0, 0)),
            pl.BlockSpec((N_EXPERTS, tm), lambda i: (0, i)),
        ],
        out_shape=[
            jax.ShapeDtypeStruct((n,) + ROW_TILE, BF16),
            jax.ShapeDtypeStruct((N_EXPERTS, n), F32),
        ],
        compiler_params=pltpu.CompilerParams(
            dimension_semantics=("arbitrary",), vmem_limit_bytes=VMEM_LIMIT),
        name="out_proj",
    )(oa, ob, h0, wa, wb, g, b, wr_hi, wr_lo)


def _first_max(cur, idx, sentinel):
    m = jnp.max(cur, axis=0, keepdims=True)
    am = jnp.min(jnp.where(cur == m, idx, sentinel), axis=0, keepdims=True)
    return m, am


def _route_kernel(lg_ref, bias_ref, eidx_ref, wts_ref, rank_ref, cnt_ref, carry_ref):
    @pl.when(pl.program_id(0) == 0)
    def _():
        carry_ref[...] = jnp.zeros_like(carry_ref)

    t = lg_ref.shape[1]
    scores = jax.nn.sigmoid(lg_ref[...])
    sel = scores + bias_ref[...]
    neg_inf = jnp.float32(-jnp.inf)

    r32 = lax.broadcasted_iota(I32, (GROUP_SIZE, t), 0)
    gs = []
    for g in range(N_GROUPS):
        blk = sel[g * GROUP_SIZE:(g + 1) * GROUP_SIZE]
        m1, i1 = _first_max(blk, r32, GROUP_SIZE)
        m2 = jnp.max(jnp.where(r32 == i1, neg_inf, blk), axis=0, keepdims=True)
        gs.append(m1 + m2)
    cur = jnp.concatenate(gs, axis=0)
    r8 = lax.broadcasted_iota(I32, (N_GROUPS, t), 0)
    keep = r8 < 0
    for _ in range(TOPK_GROUPS):
        _, ig = _first_max(cur, r8, N_GROUPS)
        hit = r8 == ig
        keep = keep | hit
        cur = jnp.where(hit, neg_inf, cur)
    cur = jnp.concatenate(
        [jnp.where(keep[g:g + 1], sel[g * GROUP_SIZE:(g + 1) * GROUP_SIZE], NEG_INF) for g in range(N_GROUPS)],
        axis=0)

    row = lax.broadcasted_iota(I32, (N_EXPERTS, t), 0)
    chosen = jnp.zeros((N_EXPERTS, t), F32)
    idxs, ws = [], []
    for _ in range(TOP_K):
        _, ik = _first_max(cur, row, N_EXPERTS)
        hit = row == ik
        ws.append(jnp.sum(jnp.where(hit, scores, 0.0), axis=0, keepdims=True))
        cur = jnp.where(hit, neg_inf, cur)
        chosen = jnp.where(hit, 1.0, chosen)
        idxs.append(ik)
    eidx = jnp.concatenate(idxs, axis=0)
    w = jnp.concatenate(ws, axis=0)
    eidx_ref[...] = eidx
    wts_ref[...] = w / jnp.sum(w, axis=0, keepdims=True) * ROUTED_SCALE

    before = lax.broadcasted_iota(I32, (t, t), 0) < lax.broadcasted_iota(I32, (t, t), 1)
    prefix = jnp.dot(chosen.astype(BF16), jnp.where(before, 1.0, 0.0).astype(BF16), preferred_element_type=F32)
    prefix = prefix + carry_ref[...]
    ranks = [jnp.sum(jnp.where(row == idxs[k], prefix, 0.0), axis=0, keepdims=True) for k in range(TOP_K)]
    rank_ref[...] = jnp.concatenate(ranks, axis=0).astype(I32)
    carry_ref[...] = carry_ref[...] + jnp.sum(chosen, axis=1, keepdims=True)
    cnt_ref[...] = carry_ref[...].astype(I32)


def _route(logits_t, bias, t):
    n = logits_t.shape[1]
    col = lambda i: (0, i)
    return pl.pallas_call(
        _route_kernel,
        grid=(n // t,),
        in_specs=[pl.BlockSpec((N_EXPERTS, t), col), _const_spec((N_EXPERTS, 1))],
        out_specs=[
            pl.BlockSpec((TOP_K, t), col), pl.BlockSpec((TOP_K, t), col), pl.BlockSpec((TOP_K, t), col),
            pl.BlockSpec((N_EXPERTS, 1), lambda i: (0, 0)),
        ],
        out_shape=[
            jax.ShapeDtypeStruct((TOP_K, n), I32),
            jax.ShapeDtypeStruct((TOP_K, n), F32),
            jax.ShapeDtypeStruct((TOP_K, n), I32),
            jax.ShapeDtypeStruct((N_EXPERTS, 1), I32),
        ],
        scratch_shapes=[pltpu.VMEM((N_EXPERTS, 1), F32)],
        compiler_params=pltpu.CompilerParams(
            dimension_semantics=("arbitrary",), vmem_limit_bytes=VMEM_LIMIT),
        name="route",
    )(logits_t, bias)


def _positions_kernel(eidx_ref, rank_ref, start_ref, pos_ref):
    t = eidx_ref.shape[1]
    row = lax.broadcasted_iota(I32, (N_EXPERTS, t), 0)
    start = start_ref[...]
    eidx = eidx_ref[...]
    base = [jnp.sum(jnp.where(row == eidx[k:k + 1], start, 0), axis=0, keepdims=True) for k in range(TOP_K)]
    pos_ref[...] = jnp.concatenate(base, axis=0) + rank_ref[...]


def _positions(eidx, rank, starts, t):
    n = eidx.shape[1]
    col = lambda i: (0, i)
    return pl.pallas_call(
        _positions_kernel,
        grid=(n // t,),
        in_specs=[pl.BlockSpec((TOP_K, t), col), pl.BlockSpec((TOP_K, t), col), _const_spec((N_EXPERTS, 1))],
        out_specs=pl.BlockSpec((None, TOP_K, t), lambda i: (i, 0, 0)),
        out_shape=jax.ShapeDtypeStruct((n // t, TOP_K, t), I32),
        compiler_params=pltpu.CompilerParams(dimension_semantics=("arbitrary",)),
        name="positions",
    )(eidx, rank, starts)


def _zero_fill(row0_ref, len_ref, nu_ref, zbuf, xs_ref, sem, wait):
    def run(copy):
        copy.wait() if wait else copy.start()

    def per_expert(e, carry):
        r0 = row0_ref[e]
        ln = len_ref[e]
        bit = MOE_BLOCK // 2
        while bit >= 1:
            off = ln - (ln & (2 * bit - 1))

            @pl.when((ln & bit) != 0)
            def _(bit=bit, off=off):
                run(pltpu.make_async_copy(zbuf.at[pl.ds(0, bit)], xs_ref.at[pl.ds(r0 + off, bit)], sem))
            bit //= 2
        return carry

    lax.fori_loop(0, N_EXPERTS, per_expert, 0)

    def per_block(g, carry):
        run(pltpu.make_async_copy(zbuf, xs_ref.at[pl.ds(g * MOE_BLOCK, MOE_BLOCK)], sem))
        return carry

    lax.fori_loop(nu_ref[0], xs_ref.shape[0] // MOE_BLOCK, per_block, 0)


def _dispatch_kernel(row0_ref, len_ref, nu_ref, pos_ref, h_ref, xs_ref, zbuf, sem, zsem):
    td = h_ref.shape[0]

    @pl.when(pl.program_id(0) == 0)
    def _():
        zbuf[...] = jnp.zeros_like(zbuf)
        _zero_fill(row0_ref, len_ref, nu_ref, zbuf, xs_ref, zsem, wait=False)

    def issue(t, carry):
        for k in range(TOP_K):
            pltpu.make_async_copy(h_ref.at[t], xs_ref.at[pos_ref[k, t]], sem).start(priority=k % 2)
        return carry

    lax.fori_loop(0, td, issue, 0, unroll=4)
    for _ in range(TOP_K):
        pltpu.make_async_copy(h_ref, xs_ref.at[pl.ds(0, td)], sem).wait()

    @pl.when(pl.program_id(0) == pl.num_programs(0) - 1)
    def _():
        _zero_fill(row0_ref, len_ref, nu_ref, zbuf, xs_ref, zsem, wait=True)


def _dispatch(pad_row0, pad_len, n_used, pos3, h1, n_rows, td):
    n = h1.shape[0]
    tp = pos3.shape[2]
    per = tp // td
    return pl.pallas_call(
        _dispatch_kernel,
        grid_spec=pltpu.PrefetchScalarGridSpec(
            num_scalar_prefetch=3,
            grid=(n // td,),
            in_specs=[
                pl.BlockSpec((None, TOP_K, td), lambda i, *_: (i // per, 0, i % per), memory_space=pltpu.SMEM),
                pl.BlockSpec((td,) + ROW_TILE, lambda i, *_: (i, 0, 0)),
            ],
            out_specs=pl.BlockSpec(memory_space=pl.ANY),
            scratch_shapes=[pltpu.VMEM((MOE_BLOCK,) + ROW_TILE, BF16), pltpu.SemaphoreType.DMA,
                            pltpu.SemaphoreType.DMA],
        ),
        out_shape=jax.ShapeDtypeStruct((n_rows,) + ROW_TILE, BF16),
        compiler_params=pltpu.CompilerParams(dimension_semantics=("arbitrary",)),
        name="dispatch",
    )(pad_row0, pad_len, n_used, pos3, h1)


CAST_CHUNK_ELEMS = 32 * 1024


def _cast_rows(src, dst):
    chunk = CAST_CHUNK_ELEMS // src.shape[1]
    for r in range(0, src.shape[0], chunk):
        dst[r:r + chunk, :] = src[r:r + chunk, :].astype(dst.dtype)


WEIGHT_DMA_PRIORITY = 1


def _moe_kernel(be_ref, nu_ref, nxt_ref, slot_ref, xs_hbm, wg_hbm, wu_hbm, wd_hbm, ys_hbm,
                xbuf, ybuf, wg_land, wu_land, wd_land, wg_s, wu_s, wd_s, xsem, ysem, wsems):
    n_used = nu_ref[0]
    n_blocks = xs_hbm.shape[0] // MOE_BLOCK

    def rows(b):
        return pl.ds(b * MOE_BLOCK, MOE_BLOCK)

    def x_copy(b, s):
        return pltpu.make_async_copy(xs_hbm.at[rows(b)], xbuf.at[s], xsem.at[s])

    def y_copy(b, s):
        return pltpu.make_async_copy(ybuf.at[s], ys_hbm.at[rows(b)], ysem.at[s])

    def weight_copies(e, s):
        return (pltpu.make_async_copy(wg_hbm.at[e], wg_land.at[s], wsems.at[s, 0]),
                pltpu.make_async_copy(wu_hbm.at[e], wu_land.at[s], wsems.at[s, 1]),
                pltpu.make_async_copy(wd_hbm.at[e], wd_land.at[s], wsems.at[s, 2]))

    x_copy(0, 0).start()
    for c in weight_copies(be_ref[0], slot_ref[0]):
        c.start(priority=WEIGHT_DMA_PRIORITY)

    def block(b, carry):
        s = b % 2
        expert = be_ref[b]
        slot = slot_ref[b]
        new_expert = (b == 0) | (expert != be_ref[jnp.maximum(b - 1, 0)])

        @pl.when(b + 1 < n_used)
        def _():
            x_copy(b + 1, 1 - s).start()

        x_copy(b, s).wait()

        @pl.when(b >= 2)
        def _():
            y_copy(b - 2, s).wait()

        def expert_mlp():
            x = xbuf[s].reshape(MOE_BLOCK, D_MODEL)
            gate = jnp.dot(x, wg_s[...], preferred_element_type=F32)
            up = jnp.dot(x, wu_s[...], preferred_element_type=F32)
            act = (_silu(gate) * up).astype(BF16)
            y = jnp.dot(act, wd_s[...], preferred_element_type=F32)
            ybuf[s] = y.astype(BF16).reshape(ybuf.shape[1:])

        @pl.when(new_expert)
        def _():
            nxt = nxt_ref[b]

            @pl.when(nxt >= 0)
            def _():
                for c in weight_copies(nxt, 1 - slot):
                    c.start(priority=WEIGHT_DMA_PRIORITY)

            for c in weight_copies(expert, slot):
                c.wait()
            _cast_rows(wg_land.at[slot], wg_s)
            _cast_rows(wu_land.at[slot], wu_s)
            _cast_rows(wd_land.at[slot], wd_s)
            expert_mlp()

        @pl.when(jnp.logical_not(new_expert))
        def _():
            expert_mlp()

        y_copy(b, s).start()
        return carry

    lax.fori_loop(0, n_used, block, 0)

    @pl.when(n_used >= 2)
    def _():
        y_copy(n_used - 2, n_used % 2).wait()

    y_copy(n_used - 1, (n_used - 1) % 2).wait()

    ybuf[0] = jnp.zeros(ybuf.shape[1:], ybuf.dtype)

    def fill(b, carry):
        y_copy(b, 0).start()
        return carry

    def fill_done(b, carry):
        y_copy(b, 0).wait()
        return carry

    lax.fori_loop(n_used, n_blocks, fill, 0)
    lax.fori_loop(n_used, n_blocks, fill_done, 0)


def _moe(block_e, n_used, next_e, slot, xs, w_gate, w_up, w_down):
    hbm = pl.BlockSpec(memory_space=pl.ANY)
    return pl.pallas_call(
        _moe_kernel,
        grid_spec=pltpu.PrefetchScalarGridSpec(
            num_scalar_prefetch=4,
            grid=(1,),
            in_specs=[hbm, hbm, hbm, hbm],
            out_specs=hbm,
            scratch_shapes=[
                pltpu.VMEM((2, MOE_BLOCK) + ROW_TILE, BF16),
                pltpu.VMEM((2, MOE_BLOCK) + ROW_TILE, BF16),
                pltpu.VMEM((2, D_MODEL, D_EXPERT), F32),
                pltpu.VMEM((2, D_MODEL, D_EXPERT), F32),
                pltpu.VMEM((2, D_EXPERT, D_MODEL), F32),
                pltpu.VMEM((D_MODEL, D_EXPERT), BF16),
                pltpu.VMEM((D_MODEL, D_EXPERT), BF16),
                pltpu.VMEM((D_EXPERT, D_MODEL), BF16),
                pltpu.SemaphoreType.DMA((2,)),
                pltpu.SemaphoreType.DMA((2,)),
                pltpu.SemaphoreType.DMA((2, 3)),
            ],
        ),
        out_shape=jax.ShapeDtypeStruct(xs.shape, xs.dtype),
        compiler_params=pltpu.CompilerParams(
            dimension_semantics=("arbitrary",), vmem_limit_bytes=VMEM_LIMIT),
        name="moe",
    )(block_e, n_used, next_e, slot, xs, w_gate, w_up, w_down)


def _combine_kernel(pos_ref, nxt_ref, h_ref, wts_ref, ys_ref, wsg_ref, wsu_ref, wsd_ref, g_ref, b_ref,
                    o_ref, gbuf_a, gbuf_b, acc_ref, sem_a, sem_b):
    tc = h_ref.shape[0] // 2
    step = pl.program_id(0)

    def issue(pos, t0, gbuf, sem):
        for t in range(tc):
            for k in range(TOP_K):
                pltpu.make_async_copy(ys_ref.at[pos[k, t0 + t]], gbuf.at[k, t], sem).start(priority=k % 2)

    def drain(gbuf, sem):
        for k in range(TOP_K):
            pltpu.make_async_copy(ys_ref.at[pl.ds(0, tc)], gbuf.at[k], sem).wait()

    def shared_expert(rows):
        hb = h_ref[rows].reshape(tc, D_MODEL)
        act = _silu(jnp.dot(hb, wsg_ref[...], preferred_element_type=F32))
        act = (act * jnp.dot(hb, wsu_ref[...], preferred_element_type=F32)).astype(BF16)
        return DN_ALPHA * hb.astype(F32) + jnp.dot(act, wsd_ref[...], preferred_element_type=F32)

    def finish(rows, base, gbuf):
        for r in range(0, tc, SUBLANES):
            wts = wts_ref[rows.start + r:rows.start + r + SUBLANES]
            acc = jnp.zeros((SUBLANES,) + ROW_TILE, F32)
            for k in range(TOP_K):
                wk = jnp.broadcast_to(wts[:, k:k + 1, :], acc.shape)
                acc = acc + wk * gbuf[k, r:r + SUBLANES].astype(F32)
            acc_ref[r:r + SUBLANES, :] = acc.reshape(SUBLANES, D_MODEL)
        o_ref[rows, :] = _layer_norm(base + acc_ref[...], g_ref[...], b_ref[...])

    first, second = slice(0, tc), slice(tc, 2 * tc)

    @pl.when(step == 0)
    def _():
        issue(pos_ref, 0, gbuf_a, sem_a)

    drain(gbuf_a, sem_a)
    finish(first, shared_expert(first), gbuf_a)
    issue(pos_ref, tc, gbuf_b, sem_b)

    drain(gbuf_b, sem_b)
    finish(second, shared_expert(second), gbuf_b)
    issue(nxt_ref, 0, gbuf_a, sem_a)

    @pl.when(step == pl.num_programs(0) - 1)
    def _():
        drain(gbuf_a, sem_a)


def _combine(pos3, h1, wts3, ys, wsg, wsu, wsd, g, b, tc):
    n = h1.shape[0]
    tp = pos3.shape[2]
    per = tp // tc
    steps = n // (2 * tc)
    row3 = lambda i: (i, 0, 0)

    def next_tile(i):
        t = jnp.minimum(2 * i + 2, 2 * steps - 1)
        return (t // per, 0, t % per)

    return pl.pallas_call(
        _combine_kernel,
        grid=(steps,),
        in_specs=[
            pl.BlockSpec((None, TOP_K, 2 * tc), lambda i: (2 * i // per, 0, i % (per // 2)), memory_space=pltpu.SMEM),
            pl.BlockSpec((None, TOP_K, tc), next_tile, memory_space=pltpu.SMEM),
            pl.BlockSpec((2 * tc,) + ROW_TILE, row3),
            pl.BlockSpec((2 * tc,) + wts3.shape[1:], row3),
            pl.BlockSpec(memory_space=pl.ANY),
            _const_spec(wsg.shape), _const_spec(wsu.shape), _const_spec(wsd.shape),
            _const_spec((1, D_MODEL)), _const_spec((1, D_MODEL)),
        ],
        out_specs=pl.BlockSpec((2 * tc, D_MODEL), lambda i: (i, 0)),
        out_shape=jax.ShapeDtypeStruct((n, D_MODEL), F32),
        scratch_shapes=[pltpu.VMEM((TOP_K, tc) + ROW_TILE, BF16), pltpu.VMEM((TOP_K, tc) + ROW_TILE, BF16),
                        pltpu.VMEM((tc, D_MODEL), F32), pltpu.SemaphoreType.DMA, pltpu.SemaphoreType.DMA],
        compiler_params=pltpu.CompilerParams(
            dimension_semantics=("arbitrary",), vmem_limit_bytes=VMEM_LIMIT),
        name="combine",
    )(pos3, pos3, h1, wts3, ys, wsg, wsu, wsd, g, b)


def _rotate_half_cols(w):
    half = w.shape[-1] // 2
    return jnp.concatenate([-w[..., half:], w[..., :half]], axis=-1)


def _rope_tables(pos):
    inv_freq = ROPE_THETA ** (-jnp.arange(0, B_ROPE, 2, dtype=F32) / B_ROPE)
    ang = pos[:, None] * inv_freq[None, :]
    cos = jnp.tile(jnp.cos(ang), (1, 2 * B_HEADS))
    sin = jnp.tile(jnp.sin(ang), (1, 2 * B_HEADS))
    return cos, sin


def kernel(x, meta_tokens, ln_emb_g, ln_emb_b, w_in, q_norm_g, w_q_up, kv_norm_g, w_kv_up, sink_logits, w_out,
           ln_mix_g, ln_mix_b, w_router, router_bias, w_exp_gate, w_exp_up, w_exp_down, w_sh_gate, w_sh_up,
           w_sh_down, ln_ffn_g, ln_ffn_b):
    batch, seq, d = x.shape
    assert d == D_MODEL and seq % 512 == 0 and w_in.shape[0] == DEPTH
    n = batch * seq
    row2 = lambda v: v.reshape(1, -1).astype(F32)

    w_in0 = w_in[0]
    kpe_cols = w_in0[:, -B_ROPE:]
    w_in_b = jnp.concatenate([w_in0, _rotate_half_cols(kpe_cols)], axis=1).astype(BF16)
    wq = w_q_up[0]
    wq_pe = wq[:, :, B_NOPE:]
    wq_b = jnp.concatenate([
        wq[:, :, :B_NOPE].reshape(B_Q_RANK, -1),
        wq_pe.reshape(B_Q_RANK, -1),
        _rotate_half_cols(wq_pe).reshape(B_Q_RANK, -1)], axis=1).astype(BF16)
    wkv = w_kv_up[0]
    wkv_b = jnp.concatenate([
        wkv[:, :, :B_NOPE].reshape(B_KV_RANK, -1), wkv[:, :, B_NOPE:].reshape(B_KV_RANK, -1)], axis=1).astype(BF16)
    w_out_b = w_out[0].astype(BF16)
    wr_t = w_router[0].astype(F32).T
    wr_hi = wr_t.astype(BF16)
    wr_lo = (wr_t - wr_hi.astype(F32)).astype(BF16)

    cos_m, sin_m = _rope_tables(jnp.arange(N_META, dtype=F32))
    cos_x, sin_x = _rope_tables(jnp.arange(N_META, N_META + seq, dtype=F32))
    rep_col = jnp.arange(A_Q_W)
    rep_src = rep_col // (A_Q_W // A_KV_HEADS) * A_HEAD_DIM + rep_col % A_HEAD_DIM
    rep = (jnp.arange(A_KV_W)[:, None] == rep_src[None, :]).astype(BF16)
    proj_args = (row2(ln_emb_g), row2(ln_emb_b), w_in_b, row2(q_norm_g[0]), wq_b, row2(kv_norm_g[0]), wkv_b, rep)
    _, _, ka_m, va_m, _, km_m, vm_m = _embed_proj(
        meta_tokens.astype(F32), 1, N_META, N_META, *proj_args, cos_m, sin_m)
    h0, qa, ka, va, qm, km, vm = _embed_proj(x.reshape(n, d), batch, seq, 256, *proj_args, cos_x, sin_x)

    pad = jnp.zeros((BLK - N_META, A_Q_W), BF16)
    oa = _win_attn(sink_logits[0].astype(F32), qa, ka, va,
                   jnp.concatenate([pad, ka_m], axis=0), jnp.concatenate([pad, va_m], axis=0), batch, seq)
    ob = _mla_attn(qm, km, vm, km_m, vm_m, min(seq, 1024))

    h1, logits_t = _out_proj(oa, ob, h0, w_out_b[:A_Q_W], w_out_b[A_Q_W:], row2(ln_mix_g[0]), row2(ln_mix_b[0]),
                              wr_hi, wr_lo, 512)

    eidx, wts, rank, counts = _route(logits_t, router_bias[0].astype(F32).reshape(N_EXPERTS, 1), 512)

    counts = counts[:, 0]
    nk = n * TOP_K
    padded = (counts + MOE_BLOCK - 1) // MOE_BLOCK * MOE_BLOCK
    pad_ends = jnp.cumsum(padded)
    pad_starts = pad_ends - padded
    n_rows = (nk + MOE_BLOCK - 1) // MOE_BLOCK * MOE_BLOCK + N_EXPERTS * MOE_BLOCK
    n_blocks = n_rows // MOE_BLOCK
    blk_row0 = jnp.arange(n_blocks, dtype=I32) * MOE_BLOCK
    block_e = jnp.minimum(jnp.sum(pad_ends[None, :] <= blk_row0[:, None], axis=1), N_EXPERTS - 1).astype(I32)
    owner = block_e[:, None] == jnp.arange(N_EXPERTS, dtype=I32)[None, :]
    per_block = lambda table: jnp.sum(jnp.where(owner, table[None, :], 0), axis=1)
    seg_ends = (pad_starts + counts).astype(I32)
    n_used = (pad_ends[-1:] // MOE_BLOCK).astype(I32)
    eid = jnp.arange(N_EXPERTS, dtype=I32)
    later_used = jnp.where((eid[None, :] > eid[:, None]) & (counts[None, :] > 0), eid[None, :], N_EXPERTS)
    next_used = jnp.min(later_used, axis=1)
    next_e = per_block(jnp.where(next_used < N_EXPERTS, next_used, -1).astype(I32)).astype(I32)
    land_slot = (per_block(jnp.cumsum((counts > 0).astype(I32)).astype(I32)) % 2).astype(I32)

    pos3 = _positions(eidx, rank, pad_starts.astype(I32).reshape(N_EXPERTS, 1), 512)
    xs = _dispatch(seg_ends, (padded - counts).astype(I32), n_used, pos3, h1, n_rows, 256)
    ys = _moe(block_e, n_used, next_e, land_slot, xs, w_exp_gate.reshape(w_exp_gate.shape[1:]),
              w_exp_up.reshape(w_exp_up.shape[1:]), w_exp_down.reshape(w_exp_down.shape[1:]))
    wts3 = jnp.broadcast_to(wts.T[:, :, None], (n, TOP_K, BLK))
    out = _combine(pos3, h1, wts3, ys, w_sh_gate[0].astype(BF16), w_sh_up[0].astype(BF16),
                   w_sh_down[0].astype(BF16), row2(ln_ffn_g[0]), row2(ln_ffn_b[0]), 128)
    return out.reshape(batch, seq, d)
```

```python
import functools

import jax
import jax.numpy as jnp
from jax import lax
from jax.experimental import pallas as pl
from jax.experimental.pallas import tpu as pltpu

F32 = jnp.float32
BF16 = jnp.bfloat16
I32 = jnp.int32

D_MODEL = 2048
N_META = 16
BLK = 128
A_HEADS = 16
A_KV_HEADS = 4
A_HEAD_DIM = 64
A_WINDOW = 128
B_HEADS = 8
B_Q_RANK = 512
B_KV_RANK = 256
B_NOPE = 128
B_ROPE = 64
B_V = 128
ROPE_THETA = 10000.0
A_Q_W = A_HEADS * A_HEAD_DIM
A_KV_W = A_KV_HEADS * A_HEAD_DIM
N_EXPERTS = 256
N_GROUPS = 8
GROUP_SIZE = N_EXPERTS // N_GROUPS
TOPK_GROUPS = 4
TOP_K = 8
D_EXPERT = 512
D_SHARED = 512
ROUTED_SCALE = 2.5
MOE_BLOCK = 256
DEPTH = 1
DN_ALPHA = (2 * DEPTH) ** 0.25
LN_EPS = 1e-5
RMS_EPS = 1e-6
NEG_INF = -1e30

A_SCALE = A_HEAD_DIM ** -0.5
B_SCALE = (B_NOPE + B_ROPE) ** -0.5
B_QK = B_NOPE + B_ROPE

SUBLANES = 8
LANES = 128
ROW_TILE = (D_MODEL // LANES, LANES)
assert ROW_TILE[0] == 2 * SUBLANES

VMEM_LIMIT = 56 * 1024 * 1024

NT_DIMS = (((1,), (1,)), ((), ()))


def _const_spec(shape):
    zeros = (0,) * len(shape)
    return pl.BlockSpec(shape, lambda *_: zeros, pipeline_mode=pl.Buffered(1))


def _layer_norm(z, g, b):
    mu = jnp.mean(z, axis=-1, keepdims=True)
    zc = z - mu
    var = jnp.mean(zc * zc, axis=-1, keepdims=True)
    return zc * lax.rsqrt(var + LN_EPS) * g + b


def _rms_norm(z, g):
    return z * lax.rsqrt(jnp.mean(z * z, axis=-1, keepdims=True) + RMS_EPS) * g


def _silu(z):
    return z * jax.nn.sigmoid(z)


def _embed_proj_kernel(x_ref, g_ref, b_ref, win_ref, gq_ref, wq_ref, gkv_ref, wkv_ref, cos_ref, sin_ref, rep_ref,
                       h0_ref, qa_ref, ka_ref, va_ref, qm_ref, km_ref, vm_ref):
    h = _layer_norm(x_ref[...], g_ref[...], b_ref[...])
    hb = h.astype(BF16)
    h0_ref[...] = hb
    p = jnp.dot(hb, win_ref[...], preferred_element_type=F32)
    o = 0
    qa_ref[...] = (p[:, o:o + A_Q_W] * A_SCALE).astype(BF16)
    o += A_Q_W
    ka_ref[...] = jnp.dot(p[:, o:o + A_KV_W].astype(BF16), rep_ref[...], preferred_element_type=F32).astype(BF16)
    o += A_KV_W
    va_ref[...] = jnp.dot(p[:, o:o + A_KV_W].astype(BF16), rep_ref[...], preferred_element_type=F32).astype(BF16)
    o += A_KV_W
    cq = p[:, o:o + B_Q_RANK]
    o += B_Q_RANK
    ckv = p[:, o:o + B_KV_RANK]
    o += B_KV_RANK
    kpe = p[:, o:o + B_ROPE]
    o += B_ROPE
    kpe_rot = p[:, o:o + B_ROPE]

    cos = cos_ref[...]
    sin = sin_ref[...]
    qall = jnp.dot(_rms_norm(cq, gq_ref[...]).astype(BF16), wq_ref[...], preferred_element_type=F32)
    n_nope = B_HEADS * B_NOPE
    n_pe = B_HEADS * B_ROPE
    q_pe = qall[:, n_nope:n_nope + n_pe] * cos + qall[:, n_nope + n_pe:] * sin
    kv = jnp.dot(_rms_norm(ckv, gkv_ref[...]).astype(BF16), wkv_ref[...], preferred_element_type=F32)
    k_pe = (kpe * cos[:, :B_ROPE] + kpe_rot * sin[:, :B_ROPE]).astype(BF16)
    for hd in range(B_HEADS):
        qn = (qall[:, hd * B_NOPE:(hd + 1) * B_NOPE] * B_SCALE).astype(BF16)
        qp = (q_pe[:, hd * B_ROPE:(hd + 1) * B_ROPE] * B_SCALE).astype(BF16)
        qm_ref[hd] = jnp.concatenate([qn, qp], axis=1)
        kn = kv[:, hd * B_NOPE:(hd + 1) * B_NOPE].astype(BF16)
        km_ref[hd] = jnp.concatenate([kn, k_pe], axis=1)
        vm_ref[hd] = kv[:, n_nope + hd * B_V:n_nope + (hd + 1) * B_V].astype(BF16)


def _embed_proj(x2, batch, seq, tm, ln_g, ln_b, w_in, gq, wq, gkv, wkv, rep, cos, sin):
    n = batch * seq
    nt = seq // tm
    row = lambda b, i: (b * nt + i, 0)
    head_major = lambda b, i: (b, 0, i, 0)
    in_w = w_in.shape[1]
    return pl.pallas_call(
        _embed_proj_kernel,
        grid=(batch, nt),
        in_specs=[
            pl.BlockSpec((tm, D_MODEL), row),
            _const_spec((1, D_MODEL)), _const_spec((1, D_MODEL)),
            _const_spec((D_MODEL, in_w)),
            _const_spec((1, B_Q_RANK)), _const_spec(wq.shape),
            _const_spec((1, B_KV_RANK)), _const_spec(wkv.shape),
            pl.BlockSpec((tm, B_HEADS * B_ROPE), lambda b, i: (i, 0)),
            pl.BlockSpec((tm, B_HEADS * B_ROPE), lambda b, i: (i, 0)),
            _const_spec(rep.shape),
        ],
        out_specs=[
            pl.BlockSpec((tm, D_MODEL), row),
            pl.BlockSpec((tm, A_Q_W), row),
            pl.BlockSpec((tm, A_Q_W), row),
            pl.BlockSpec((tm, A_Q_W), row),
            pl.BlockSpec((None, B_HEADS, tm, B_QK), head_major),
            pl.BlockSpec((None, B_HEADS, tm, B_QK), head_major),
            pl.BlockSpec((None, B_HEADS, tm, B_V), head_major),
        ],
        out_shape=[
            jax.ShapeDtypeStruct((n, D_MODEL), BF16),
            jax.ShapeDtypeStruct((n, A_Q_W), BF16),
            jax.ShapeDtypeStruct((n, A_Q_W), BF16),
            jax.ShapeDtypeStruct((n, A_Q_W), BF16),
            jax.ShapeDtypeStruct((batch, B_HEADS, seq, B_QK), BF16),
            jax.ShapeDtypeStruct((batch, B_HEADS, seq, B_QK), BF16),
            jax.ShapeDtypeStruct((batch, B_HEADS, seq, B_V), BF16),
        ],
        compiler_params=pltpu.CompilerParams(
            dimension_semantics=("arbitrary", "arbitrary"), vmem_limit_bytes=VMEM_LIMIT),
        name="embed_proj",
    )(x2, ln_g, ln_b, w_in, gq, wq, gkv, wkv, cos, sin, rep)


def _win_attn_kernel(sink_ref, q_ref, kp_ref, kc_ref, kn_ref, vp_ref, vc_ref, vn_ref, kmeta_ref, vmeta_ref,
                     bias_ref, o_ref):
    first = pl.program_id(1) == 0
    grp = A_HEADS // A_KV_HEADS
    gw = grp * A_HEAD_DIM
    k3 = jnp.concatenate([jnp.where(first, kmeta_ref[...], kp_ref[...]), kc_ref[...], kn_ref[...]], axis=0)
    v3 = jnp.concatenate([jnp.where(first, vmeta_ref[...], vp_ref[...]), vc_ref[...], vn_ref[...]], axis=0)
    lane_head = lax.broadcasted_iota(I32, (BLK, gw), 1) // A_HEAD_DIM
    row_head = lax.broadcasted_iota(I32, (grp * BLK, 1), 0) // BLK
    for kvh in range(A_KV_HEADS):
        cols = slice(kvh * gw, (kvh + 1) * gw)
        qg = q_ref[:, cols]
        q4 = jnp.concatenate([jnp.where(lane_head == r, qg, jnp.zeros_like(qg)) for r in range(grp)], axis=0)
        sink = jnp.zeros((grp * BLK, 1), F32)
        for r in range(grp):
            sink = jnp.where(row_head == r, sink_ref[kvh * grp + r], sink)
        s = lax.dot_general(q4, k3[:, cols], NT_DIMS, preferred_element_type=F32) + bias_ref[kvh]
        m = jnp.maximum(jnp.max(s, axis=-1, keepdims=True), sink)
        p = jnp.exp(s - m)
        denom = jnp.sum(p, axis=-1, keepdims=True) + jnp.exp(sink - m)
        ow = jnp.dot(p.astype(BF16), v3[:, cols], preferred_element_type=F32) / denom
        og = jnp.zeros((BLK, gw), F32)
        for r in range(grp):
            og = jnp.where(lane_head == r, ow[r * BLK:(r + 1) * BLK], og)
        o_ref[:, cols] = og.astype(BF16)


def _win_bias():
    qi = jnp.arange(BLK)[:, None]
    ki = jnp.arange(3 * BLK)[None, :]
    dist = jnp.abs(BLK + qi - ki)
    in_window = dist <= A_WINDOW
    valid = jnp.stack([in_window, in_window & (ki >= BLK - N_META), in_window & (ki < 2 * BLK)])
    slopes = jnp.exp2(-8.0 * jnp.arange(1, A_HEADS + 1, dtype=F32) / A_HEADS)
    bias = jnp.where(valid[:, None], -slopes[None, :, None, None] * dist.astype(F32)[None, None], NEG_INF)
    return bias.reshape(3, A_KV_HEADS, (A_HEADS // A_KV_HEADS) * BLK, 3 * BLK)


def _win_attn(sink, qa, ka, va, kmeta, vmeta, batch, seq):
    nb = seq // BLK
    n = batch * seq
    bias = _win_bias()
    cur = lambda b, i, *_: (b * nb + i, 0)
    prev = lambda b, i, *_: (b * nb + jnp.maximum(i - 1, 0), 0)
    nxt = lambda b, i, *_: (b * nb + jnp.minimum(i + 1, nb - 1), 0)
    kv_spec = lambda f: pl.BlockSpec((BLK, A_Q_W), f)
    const = lambda b, i, *_: (0, 0)
    variant = lambda b, i, *_: (jnp.where(i == 0, 1, jnp.where(i == nb - 1, 2, 0)), 0, 0, 0)
    return pl.pallas_call(
        _win_attn_kernel,
        grid_spec=pltpu.PrefetchScalarGridSpec(
            num_scalar_prefetch=1,
            grid=(batch, nb),
            in_specs=[
                pl.BlockSpec((BLK, A_Q_W), cur),
                kv_spec(prev), kv_spec(cur), kv_spec(nxt),
                kv_spec(prev), kv_spec(cur), kv_spec(nxt),
                pl.BlockSpec((BLK, A_Q_W), const), pl.BlockSpec((BLK, A_Q_W), const),
                pl.BlockSpec((None,) + bias.shape[1:], variant),
            ],
            out_specs=pl.BlockSpec((BLK, A_Q_W), cur),
        ),
        out_shape=jax.ShapeDtypeStruct((n, A_Q_W), BF16),
        compiler_params=pltpu.CompilerParams(
            dimension_semantics=("arbitrary", "arbitrary"), vmem_limit_bytes=VMEM_LIMIT),
        name="win_attn",
    )(sink, qa, ka, ka, ka, va, va, va, kmeta, vmeta, bias)


MLA_KEY_CHUNKS = 2


def _mla_attn_kernel(q_ref, k_ref, v_ref, km_ref, vm_ref, o_ref):
    q = q_ref[...]
    sm = lax.dot_general(q, km_ref[...], NT_DIMS, preferred_element_type=F32)
    m = jnp.max(sm, axis=-1, keepdims=True)
    pm = jnp.exp(sm - m)
    denom = jnp.sum(pm, axis=-1, keepdims=True)
    acc = jnp.dot(pm.astype(BF16), vm_ref[...], preferred_element_type=F32)
    chunk = k_ref.shape[0] // MLA_KEY_CHUNKS
    for c in range(MLA_KEY_CHUNKS):
        keys = slice(c * chunk, (c + 1) * chunk)
        s = lax.dot_general(q, k_ref[keys, :], NT_DIMS, preferred_element_type=F32)
        m_new = jnp.maximum(m, jnp.max(s, axis=-1, keepdims=True))
        rescale = jnp.exp(m - m_new)
        p = jnp.exp(s - m_new)
        denom = rescale * denom + jnp.sum(p, axis=-1, keepdims=True)
        acc = rescale * acc + jnp.dot(p.astype(BF16), v_ref[keys, :], preferred_element_type=F32)
        m = m_new
    o_ref[...] = (acc / denom).astype(BF16)


def _mla_attn(qm, km, vm, km_meta, vm_meta, tq):
    batch, heads, seq, _ = qm.shape
    nq = seq // tq
    return pl.pallas_call(
        _mla_attn_kernel,
        grid=(batch, heads, nq),
        in_specs=[
            pl.BlockSpec((None, None, tq, B_QK), lambda b, h, i: (b, h, i, 0)),
            pl.BlockSpec((None, None, seq, B_QK), lambda b, h, i: (b, h, 0, 0)),
            pl.BlockSpec((None, None, seq, B_V), lambda b, h, i: (b, h, 0, 0)),
            pl.BlockSpec((None, None, N_META, B_QK), lambda b, h, i: (0, h, 0, 0)),
            pl.BlockSpec((None, None, N_META, B_V), lambda b, h, i: (0, h, 0, 0)),
        ],
        out_specs=pl.BlockSpec((tq, B_V), lambda b, h, i: (b * nq + i, h)),
        out_shape=jax.ShapeDtypeStruct((batch * seq, heads * B_V), BF16),
        compiler_params=pltpu.CompilerParams(
            dimension_semantics=("arbitrary", "arbitrary", "arbitrary"), vmem_limit_bytes=VMEM_LIMIT),
        name="mla_attn",
    )(qm, km, vm, km_meta, vm_meta)


def _out_proj_kernel(oa_ref, ob_ref, h0_ref, wa_ref, wb_ref, g_ref, b_ref, wr_hi_ref, wr_lo_ref,
                     h1_ref, lg_ref):
    mix = jnp.dot(oa_ref[...], wa_ref[...], preferred_element_type=F32)
    mix = mix + jnp.dot(ob_ref[...], wb_ref[...], preferred_element_type=F32)
    h1 = _layer_norm(DN_ALPHA * h0_ref[...].astype(F32) + mix, g_ref[...], b_ref[...])
    hi = h1.astype(BF16)
    hi_f = hi.astype(F32)
    lo = (h1 - hi_f).astype(BF16)
    lg = lax.dot_general(wr_hi_ref[...], hi, NT_DIMS, preferred_element_type=F32)
    lg = lg + lax.dot_general(wr_hi_ref[...], lo, NT_DIMS, preferred_element_type=F32)
    lg = lg + lax.dot_general(wr_lo_ref[...], hi, NT_DIMS, preferred_element_type=F32)
    lg_ref[...] = lg
    h1_ref[...] = hi.reshape(h1_ref.shape)


def _out_proj(oa, ob, h0, wa, wb, g, b, wr_hi, wr_lo, tm):
    n = oa.shape[0]
    row = lambda i: (i, 0)
    return pl.pallas_call(
        _out_proj_kernel,
        grid=(n // tm,),
        in_specs=[
            pl.BlockSpec((tm, A_Q_W), row),
            pl.BlockSpec((tm, B_HEADS * B_V), row),
            pl.BlockSpec((tm, D_MODEL), row),
            _const_spec(wa.shape), _const_spec(wb.shape),
            _const_spec((1, D_MODEL)), _const_spec((1, D_MODEL)),
            _const_spec(wr_hi.shape), _const_spec(wr_lo.shape),
        ],
        out_specs=[
            pl.BlockSpec((tm,) + ROW_TILE, lambda i: (i, 0, 0)),
            pl.BlockSpec((N_EXPERTS, tm), lambda i: (0, i)),
        ],
        out_shape=[
            jax.ShapeDtypeStruct((n,) + ROW_TILE, BF16),
            jax.ShapeDtypeStruct((N_EXPERTS, n), F32),
        ],
        compiler_params=pltpu.CompilerParams(
            dimension_semantics=("arbitrary",), vmem_limit_bytes=VMEM_LIMIT),
        name="out_proj",
    )(oa, ob, h0, wa, wb, g, b, wr_hi, wr_lo)


def _first_max(cur, idx, sentinel):
    m = jnp.max(cur, axis=0, keepdims=True)
    am = jnp.min(jnp.where(cur == m, idx, sentinel), axis=0, keepdims=True)
    return m, am


def _route_kernel(lg_ref, bias_ref, eidx_ref, wts_ref, rank_ref, cnt_ref, carry_ref):
    @pl.when(pl.program_id(0) == 0)
    def _():
        carry_ref[...] = jnp.zeros_like(carry_ref)

    t = lg_ref.shape[1]
    scores = jax.nn.sigmoid(lg_ref[...])
    sel = scores + bias_ref[...]
    neg_inf = jnp.float32(-jnp.inf)

    r32 = lax.broadcasted_iota(I32, (GROUP_SIZE, t), 0)
    gs = []
    for g in range(N_GROUPS):
        blk = sel[g * GROUP_SIZE:(g + 1) * GROUP_SIZE]
        m1, i1 = _first_max(blk, r32, GROUP_SIZE)
        m2 = jnp.max(jnp.where(r32 == i1, neg_inf, blk), axis=0, keepdims=True)
        gs.append(m1 + m2)
    cur = jnp.concatenate(gs, axis=0)
    r8 = lax.broadcasted_iota(I32, (N_GROUPS, t), 0)
    keep = r8 < 0
    for _ in range(TOPK_GROUPS):
        _, ig = _first_max(cur, r8, N_GROUPS)
        hit = r8 == ig
        keep = keep | hit
        cur = jnp.where(hit, neg_inf, cur)
    cur = jnp.concatenate(
        [jnp.where(keep[g:g + 1], sel[g * GROUP_SIZE:(g + 1) * GROUP_SIZE], NEG_INF) for g in range(N_GROUPS)],
        axis=0)

    row = lax.broadcasted_iota(I32, (N_EXPERTS, t), 0)
    chosen = jnp.zeros((N_EXPERTS, t), F32)
    idxs, ws = [], []
    for _ in range(TOP_K):
        _, ik = _first_max(cur, row, N_EXPERTS)
        hit = row == ik
        ws.append(jnp.sum(jnp.where(hit, scores, 0.0), axis=0, keepdims=True))
        cur = jnp.where(hit, neg_inf, cur)
        chosen = jnp.where(hit, 1.0, chosen)
        idxs.append(ik)
    eidx = jnp.concatenate(idxs, axis=0)
    w = jnp.concatenate(ws, axis=0)
    eidx_ref[...] = eidx
    wts_ref[...] = w / jnp.sum(w, axis=0, keepdims=True) * ROUTED_SCALE

    before = lax.broadcasted_iota(I32, (t, t), 0) < lax.broadcasted_iota(I32, (t, t), 1)
    prefix = jnp.dot(chosen.astype(BF16), jnp.where(before, 1.0, 0.0).astype(BF16), preferred_element_type=F32)
    prefix = prefix + carry_ref[...]
    ranks = [jnp.sum(jnp.where(row == idxs[k], prefix, 0.0), axis=0, keepdims=True) for k in range(TOP_K)]
    rank_ref[...] = jnp.concatenate(ranks, axis=0).astype(I32)
    carry_ref[...] = carry_ref[...] + jnp.sum(chosen, axis=1, keepdims=True)
    cnt_ref[...] = carry_ref[...].astype(I32)


def _route(logits_t, bias, t):
    n = logits_t.shape[1]
    col = lambda i: (0, i)
    return pl.pallas_call(
        _route_kernel,
        grid=(n // t,),
        in_specs=[pl.BlockSpec((N_EXPERTS, t), col), _const_spec((N_EXPERTS, 1))],
        out_specs=[
            pl.BlockSpec((TOP_K, t), col), pl.BlockSpec((TOP_K, t), col), pl.BlockSpec((TOP_K, t), col),
            pl.BlockSpec((N_EXPERTS, 1), lambda i: (0, 0)),
        ],
        out_shape=[
            jax.ShapeDtypeStruct((TOP_K, n), I32),
            jax.ShapeDtypeStruct((TOP_K, n), F32),
            jax.ShapeDtypeStruct((TOP_K, n), I32),
            jax.ShapeDtypeStruct((N_EXPERTS, 1), I32),
        ],
        scratch_shapes=[pltpu.VMEM((N_EXPERTS, 1), F32)],
        compiler_params=pltpu.CompilerParams(
            dimension_semantics=("arbitrary",), vmem_limit_bytes=VMEM_LIMIT),
        name="route",
    )(logits_t, bias)


def _positions_kernel(eidx_ref, rank_ref, start_ref, pos_ref):
    t = eidx_ref.shape[1]
    row = lax.broadcasted_iota(I32, (N_EXPERTS, t), 0)
    start = start_ref[...]
    eidx = eidx_ref[...]
    base = [jnp.sum(jnp.where(row == eidx[k:k + 1], start, 0), axis=0, keepdims=True) for k in range(TOP_K)]
    pos_ref[...] = jnp.concatenate(base, axis=0) + rank_ref[...]


def _positions(eidx, rank, starts, t):
    n = eidx.shape[1]
    col = lambda i: (0, i)
    return pl.pallas_call(
        _positions_kernel,
        grid=(n // t,),
        in_specs=[pl.BlockSpec((TOP_K, t), col), pl.BlockSpec((TOP_K, t), col), _const_spec((N_EXPERTS, 1))],
        out_specs=pl.BlockSpec((None, TOP_K, t), lambda i: (i, 0, 0)),
        out_shape=jax.ShapeDtypeStruct((n // t, TOP_K, t), I32),
        compiler_params=pltpu.CompilerParams(dimension_semantics=("arbitrary",)),
        name="positions",
    )(eidx, rank, starts)


def _zero_fill(row0_ref, len_ref, nu_ref, zbuf, xs_ref, sem, wait):
    def run(copy):
        copy.wait() if wait else copy.start()

    def per_expert(e, carry):
        r0 = row0_ref[e]
        ln = len_ref[e]
        bit = MOE_BLOCK // 2
        while bit >= 1:
            off = ln - (ln & (2 * bit - 1))

            @pl.when((ln & bit) != 0)
            def _(bit=bit, off=off):
                run(pltpu.make_async_copy(zbuf.at[pl.ds(0, bit)], xs_ref.at[pl.ds(r0 + off, bit)], sem))
            bit //= 2
        return carry

    lax.fori_loop(0, N_EXPERTS, per_expert, 0)

    def per_block(g, carry):
        run(pltpu.make_async_copy(zbuf, xs_ref.at[pl.ds(g * MOE_BLOCK, MOE_BLOCK)], sem))
        return carry

    lax.fori_loop(nu_ref[0], xs_ref.shape[0] // MOE_BLOCK, per_block, 0)


def _dispatch_kernel(row0_ref, len_ref, nu_ref, pos_ref, h_ref, xs_ref, zbuf, sem, zsem):
    td = h_ref.shape[0]

    @pl.when(pl.program_id(0) == 0)
    def _():
        zbuf[...] = jnp.zeros_like(zbuf)
        _zero_fill(row0_ref, len_ref, nu_ref, zbuf, xs_ref, zsem, wait=False)

    def issue(t, carry):
        for k in range(TOP_K):
            pltpu.make_async_copy(h_ref.at[t], xs_ref.at[pos_ref[k, t]], sem).start(priority=k % 2)
        return carry

    lax.fori_loop(0, td, issue, 0, unroll=4)
    for _ in range(TOP_K):
        pltpu.make_async_copy(h_ref, xs_ref.at[pl.ds(0, td)], sem).wait()

    @pl.when(pl.program_id(0) == pl.num_programs(0) - 1)
    def _():
        _zero_fill(row0_ref, len_ref, nu_ref, zbuf, xs_ref, zsem, wait=True)


def _dispatch(pad_row0, pad_len, n_used, pos3, h1, n_rows, td):
    n = h1.shape[0]
    tp = pos3.shape[2]
    per = tp // td
    return pl.pallas_call(
        _dispatch_kernel,
        grid_spec=pltpu.PrefetchScalarGridSpec(
            num_scalar_prefetch=3,
            grid=(n // td,),
            in_specs=[
                pl.BlockSpec((None, TOP_K, td), lambda i, *_: (i // per, 0, i % per), memory_space=pltpu.SMEM),
                pl.BlockSpec((td,) + ROW_TILE, lambda i, *_: (i, 0, 0)),
            ],
            out_specs=pl.BlockSpec(memory_space=pl.ANY),
            scratch_shapes=[pltpu.VMEM((MOE_BLOCK,) + ROW_TILE, BF16), pltpu.SemaphoreType.DMA,
                            pltpu.SemaphoreType.DMA],
        ),
        out_shape=jax.ShapeDtypeStruct((n_rows,) + ROW_TILE, BF16),
        compiler_params=pltpu.CompilerParams(dimension_semantics=("arbitrary",)),
        name="dispatch",
    )(pad_row0, pad_len, n_used, pos3, h1)


CAST_CHUNK_ELEMS = 32 * 1024


def _cast_rows(src, dst):
    chunk = CAST_CHUNK_ELEMS // src.shape[1]
    for r in range(0, src.shape[0], chunk):
        dst[r:r + chunk, :] = src[r:r + chunk, :].astype(dst.dtype)


WEIGHT_DMA_PRIORITY = 1


def _moe_kernel(be_ref, nu_ref, nxt_ref, slot_ref, xs_hbm, wg_hbm, wu_hbm, wd_hbm, ys_hbm,
                xbuf, ybuf, wg_land, wu_land, wd_land, wg_s, wu_s, wd_s, xsem, ysem, wsems):
    n_used = nu_ref[0]
    n_blocks = xs_hbm.shape[0] // MOE_BLOCK

    def rows(b):
        return pl.ds(b * MOE_BLOCK, MOE_BLOCK)

    def x_copy(b, s):
        return pltpu.make_async_copy(xs_hbm.at[rows(b)], xbuf.at[s], xsem.at[s])

    def y_copy(b, s):
        return pltpu.make_async_copy(ybuf.at[s], ys_hbm.at[rows(b)], ysem.at[s])

    def weight_copies(e, s):
        return (pltpu.make_async_copy(wg_hbm.at[e], wg_land.at[s], wsems.at[s, 0]),
                pltpu.make_async_copy(wu_hbm.at[e], wu_land.at[s], wsems.at[s, 1]),
                pltpu.make_async_copy(wd_hbm.at[e], wd_land.at[s], wsems.at[s, 2]))

    x_copy(0, 0).start()
    for c in weight_copies(be_ref[0], slot_ref[0]):
        c.start(priority=WEIGHT_DMA_PRIORITY)

    def block(b, carry):
        s = b % 2
        expert = be_ref[b]
        slot = slot_ref[b]
        new_expert = (b == 0) | (expert != be_ref[jnp.maximum(b - 1, 0)])

        @pl.when(b + 1 < n_used)
        def _():
            x_copy(b + 1, 1 - s).start()

        x_copy(b, s).wait()

        @pl.when(b >= 2)
        def _():
            y_copy(b - 2, s).wait()

        def expert_mlp():
            x = xbuf[s].reshape(MOE_BLOCK, D_MODEL)
            gate = jnp.dot(x, wg_s[...], preferred_element_type=F32)
            up = jnp.dot(x, wu_s[...], preferred_element_type=F32)
            act = (_silu(gate) * up).astype(BF16)
            y = jnp.dot(act, wd_s[...], preferred_element_type=F32)
            ybuf[s] = y.astype(BF16).reshape(ybuf.shape[1:])

        @pl.when(new_expert)
        def _():
            nxt = nxt_ref[b]

            @pl.when(nxt >= 0)
            def _():
                for c in weight_copies(nxt, 1 - slot):
                    c.start(priority=WEIGHT_DMA_PRIORITY)

            for c in weight_copies(expert, slot):
                c.wait()
            _cast_rows(wg_land.at[slot], wg_s)
            _cast_rows(wu_land.at[slot], wu_s)
            _cast_rows(wd_land.at[slot], wd_s)
            expert_mlp()

        @pl.when(jnp.logical_not(new_expert))
        def _():
            expert_mlp()

        y_copy(b, s).start()
        return carry

    lax.fori_loop(0, n_used, block, 0)

    @pl.when(n_used >= 2)
    def _():
        y_copy(n_used - 2, n_used % 2).wait()

    y_copy(n_used - 1, (n_used - 1) % 2).wait()

    ybuf[0] = jnp.zeros(ybuf.shape[1:], ybuf.dtype)

    def fill(b, carry):
        y_copy(b, 0).start()
        return carry

    def fill_done(b, carry):
        y_copy(b, 0).wait()
        return carry

    lax.fori_loop(n_used, n_blocks, fill, 0)
    lax.fori_loop(n_used, n_blocks, fill_done, 0)


def _moe(block_e, n_used, next_e, slot, xs, w_gate, w_up, w_down):
    hbm = pl.BlockSpec(memory_space=pl.ANY)
    return pl.pallas_call(
        _moe_kernel,
        grid_spec=pltpu.PrefetchScalarGridSpec(
            num_scalar_prefetch=4,
            grid=(1,),
            in_specs=[hbm, hbm, hbm, hbm],
            out_specs=hbm,
            scratch_shapes=[
                pltpu.VMEM((2, MOE_BLOCK) + ROW_TILE, BF16),
                pltpu.VMEM((2, MOE_BLOCK) + ROW_TILE, BF16),
                pltpu.VMEM((2, D_MODEL, D_EXPERT), F32),
                pltpu.VMEM((2, D_MODEL, D_EXPERT), F32),
                pltpu.VMEM((2, D_EXPERT, D_MODEL), F32),
                pltpu.VMEM((D_MODEL, D_EXPERT), BF16),
                pltpu.VMEM((D_MODEL, D_EXPERT), BF16),
                pltpu.VMEM((D_EXPERT, D_MODEL), BF16),
                pltpu.SemaphoreType.DMA((2,)),
                pltpu.SemaphoreType.DMA((2,)),
                pltpu.SemaphoreType.DMA((2, 3)),
            ],
        ),
        out_shape=jax.ShapeDtypeStruct(xs.shape, xs.dtype),
        compiler_params=pltpu.CompilerParams(
            dimension_semantics=("arbitrary",), vmem_limit_bytes=VMEM_LIMIT),
        name="moe",
    )(block_e, n_used, next_e, slot, xs, w_gate, w_up, w_down)


def _combine_kernel(pos_ref, nxt_ref, h_ref, wts_ref, ys_ref, wsg_ref, wsu_ref, wsd_ref, g_ref, b_ref,
                    o_ref, *scratch):
    gbufs, (acc_ref, sems) = scratch[:COMBINE_TILES], scratch[COMBINE_TILES:]
    tc = h_ref.shape[0] // COMBINE_TILES
    step = pl.program_id(0)

    def issue(pos, t0, gbuf, sem):
        for t in range(tc):
            for k in range(TOP_K):
                pltpu.make_async_copy(ys_ref.at[pos[k, t0 + t]], gbuf.at[k, t], sem).start(priority=k % 2)

    def drain(gbuf, sem):
        for k in range(TOP_K):
            pltpu.make_async_copy(ys_ref.at[pl.ds(0, tc)], gbuf.at[k], sem).wait()

    def shared_expert(rows):
        hb = h_ref[rows].reshape(tc, D_MODEL)
        act = _silu(jnp.dot(hb, wsg_ref[...], preferred_element_type=F32))
        act = (act * jnp.dot(hb, wsu_ref[...], preferred_element_type=F32)).astype(BF16)
        return DN_ALPHA * hb.astype(F32) + jnp.dot(act, wsd_ref[...], preferred_element_type=F32)

    def finish(rows, base, gbuf):
        for r in range(0, tc, SUBLANES):
            wts = wts_ref[rows.start + r:rows.start + r + SUBLANES]
            acc = jnp.zeros((SUBLANES,) + ROW_TILE, F32)
            for k in range(TOP_K):
                wk = jnp.broadcast_to(wts[:, k:k + 1, :], acc.shape)
                acc = acc + wk * gbuf[k, r:r + SUBLANES].astype(F32)
            acc_ref[r:r + SUBLANES, :] = acc.reshape(SUBLANES, D_MODEL)
        o_ref[rows, :] = _layer_norm(base + acc_ref[...], g_ref[...], b_ref[...])

    @pl.when(step == 0)
    def _():
        for j in range(COMBINE_AHEAD):
            issue(pos_ref, j * tc, gbufs[j], sems.at[j])

    for j in range(COMBINE_TILES):
        rows = slice(j * tc, (j + 1) * tc)
        drain(gbufs[j], sems.at[j])
        finish(rows, shared_expert(rows), gbufs[j])
        ahead = j + COMBINE_AHEAD
        if ahead < COMBINE_TILES:
            issue(pos_ref, ahead * tc, gbufs[ahead], sems.at[ahead])
        else:
            issue(nxt_ref, (ahead - COMBINE_TILES) * tc, gbufs[ahead - COMBINE_TILES],
                  sems.at[ahead - COMBINE_TILES])

    @pl.when(step == pl.num_programs(0) - 1)
    def _():
        for j in range(COMBINE_AHEAD):
            drain(gbufs[j], sems.at[j])


COMBINE_TILES = 4
COMBINE_AHEAD = 2


def _combine(pos3, h1, wts3, ys, wsg, wsu, wsd, g, b, tc):
    n = h1.shape[0]
    rows = COMBINE_TILES * tc
    assert pos3.shape[2] == rows and n % rows == 0
    steps = n // rows
    row3 = lambda i: (i, 0, 0)
    return pl.pallas_call(
        _combine_kernel,
        grid=(steps,),
        in_specs=[
            pl.BlockSpec((None, TOP_K, rows), row3, memory_space=pltpu.SMEM),
            pl.BlockSpec((None, TOP_K, COMBINE_AHEAD * tc), lambda i: (jnp.minimum(i + 1, steps - 1), 0, 0),
                         memory_space=pltpu.SMEM),
            pl.BlockSpec((rows,) + ROW_TILE, row3),
            pl.BlockSpec((rows,) + wts3.shape[1:], row3),
            pl.BlockSpec(memory_space=pl.ANY),
            _const_spec(wsg.shape), _const_spec(wsu.shape), _const_spec(wsd.shape),
            _const_spec((1, D_MODEL)), _const_spec((1, D_MODEL)),
        ],
        out_specs=pl.BlockSpec((rows, D_MODEL), lambda i: (i, 0)),
        out_shape=jax.ShapeDtypeStruct((n, D_MODEL), F32),
        scratch_shapes=[pltpu.VMEM((TOP_K, tc) + ROW_TILE, BF16)] * COMBINE_TILES + [
            pltpu.VMEM((tc, D_MODEL), F32), pltpu.SemaphoreType.DMA((COMBINE_TILES,))],
        compiler_params=pltpu.CompilerParams(
            dimension_semantics=("arbitrary",), vmem_limit_bytes=VMEM_LIMIT),
        name="combine",
    )(pos3, pos3, h1, wts3, ys, wsg, wsu, wsd, g, b)


def _rotate_half_cols(w):
    half = w.shape[-1] // 2
    return jnp.concatenate([-w[..., half:], w[..., :half]], axis=-1)


def _rope_tables(pos):
    inv_freq = ROPE_THETA ** (-jnp.arange(0, B_ROPE, 2, dtype=F32) / B_ROPE)
    ang = pos[:, None] * inv_freq[None, :]
    cos = jnp.tile(jnp.cos(ang), (1, 2 * B_HEADS))
    sin = jnp.tile(jnp.sin(ang), (1, 2 * B_HEADS))
    return cos, sin


def kernel(x, meta_tokens, ln_emb_g, ln_emb_b, w_in, q_norm_g, w_q_up, kv_norm_g, w_kv_up, sink_logits, w_out,
           ln_mix_g, ln_mix_b, w_router, router_bias, w_exp_gate, w_exp_up, w_exp_down, w_sh_gate, w_sh_up,
           w_sh_down, ln_ffn_g, ln_ffn_b):
    batch, seq, d = x.shape
    assert d == D_MODEL and seq % 512 == 0 and w_in.shape[0] == DEPTH
    n = batch * seq
    row2 = lambda v: v.reshape(1, -1).astype(F32)

    w_in0 = w_in[0]
    kpe_cols = w_in0[:, -B_ROPE:]
    w_in_b = jnp.concatenate([w_in0, _rotate_half_cols(kpe_cols)], axis=1).astype(BF16)
    wq = w_q_up[0]
    wq_pe = wq[:, :, B_NOPE:]
    wq_b = jnp.concatenate([
        wq[:, :, :B_NOPE].reshape(B_Q_RANK, -1),
        wq_pe.reshape(B_Q_RANK, -1),
        _rotate_half_cols(wq_pe).reshape(B_Q_RANK, -1)], axis=1).astype(BF16)
    wkv = w_kv_up[0]
    wkv_b = jnp.concatenate([
        wkv[:, :, :B_NOPE].reshape(B_KV_RANK, -1), wkv[:, :, B_NOPE:].reshape(B_KV_RANK, -1)], axis=1).astype(BF16)
    w_out_b = w_out[0].astype(BF16)
    wr_t = w_router[0].astype(F32).T
    wr_hi = wr_t.astype(BF16)
    wr_lo = (wr_t - wr_hi.astype(F32)).astype(BF16)

    cos_m, sin_m = _rope_tables(jnp.arange(N_META, dtype=F32))
    cos_x, sin_x = _rope_tables(jnp.arange(N_META, N_META + seq, dtype=F32))
    rep_col = jnp.arange(A_Q_W)
    rep_src = rep_col // (A_Q_W // A_KV_HEADS) * A_HEAD_DIM + rep_col % A_HEAD_DIM
    rep = (jnp.arange(A_KV_W)[:, None] == rep_src[None, :]).astype(BF16)
    proj_args = (row2(ln_emb_g), row2(ln_emb_b), w_in_b, row2(q_norm_g[0]), wq_b, row2(kv_norm_g[0]), wkv_b, rep)
    _, _, ka_m, va_m, _, km_m, vm_m = _embed_proj(
        meta_tokens.astype(F32), 1, N_META, N_META, *proj_args, cos_m, sin_m)
    h0, qa, ka, va, qm, km, vm = _embed_proj(x.reshape(n, d), batch, seq, 256, *proj_args, cos_x, sin_x)

    pad = jnp.zeros((BLK - N_META, A_Q_W), BF16)
    oa = _win_attn(sink_logits[0].astype(F32), qa, ka, va,
                   jnp.concatenate([pad, ka_m], axis=0), jnp.concatenate([pad, va_m], axis=0), batch, seq)
    ob = _mla_attn(qm, km, vm, km_m, vm_m, min(seq, 1024))

    h1, logits_t = _out_proj(oa, ob, h0, w_out_b[:A_Q_W], w_out_b[A_Q_W:], row2(ln_mix_g[0]), row2(ln_mix_b[0]),
                              wr_hi, wr_lo, 512)

    eidx, wts, rank, counts = _route(logits_t, router_bias[0].astype(F32).reshape(N_EXPERTS, 1), 512)

    counts = counts[:, 0]
    nk = n * TOP_K
    padded = (counts + MOE_BLOCK - 1) // MOE_BLOCK * MOE_BLOCK
    pad_ends = jnp.cumsum(padded)
    pad_starts = pad_ends - padded
    n_rows = (nk + MOE_BLOCK - 1) // MOE_BLOCK * MOE_BLOCK + N_EXPERTS * MOE_BLOCK
    n_blocks = n_rows // MOE_BLOCK
    blk_row0 = jnp.arange(n_blocks, dtype=I32) * MOE_BLOCK
    block_e = jnp.minimum(jnp.sum(pad_ends[None, :] <= blk_row0[:, None], axis=1), N_EXPERTS - 1).astype(I32)
    owner = block_e[:, None] == jnp.arange(N_EXPERTS, dtype=I32)[None, :]
    per_block = lambda table: jnp.sum(jnp.where(owner, table[None, :], 0), axis=1)
    seg_ends = (pad_starts + counts).astype(I32)
    n_used = (pad_ends[-1:] // MOE_BLOCK).astype(I32)
    eid = jnp.arange(N_EXPERTS, dtype=I32)
    later_used = jnp.where((eid[None, :] > eid[:, None]) & (counts[None, :] > 0), eid[None, :], N_EXPERTS)
    next_used = jnp.min(later_used, axis=1)
    next_e = per_block(jnp.where(next_used < N_EXPERTS, next_used, -1).astype(I32)).astype(I32)
    land_slot = (per_block(jnp.cumsum((counts > 0).astype(I32)).astype(I32)) % 2).astype(I32)

    pos3 = _positions(eidx, rank, pad_starts.astype(I32).reshape(N_EXPERTS, 1), 512)
    xs = _dispatch(seg_ends, (padded - counts).astype(I32), n_used, pos3, h1, n_rows, 256)
    ys = _moe(block_e, n_used, next_e, land_slot, xs, w_exp_gate.reshape(w_exp_gate.shape[1:]),
              w_exp_up.reshape(w_exp_up.shape[1:]), w_exp_down.reshape(w_exp_down.shape[1:]))
    wts3 = jnp.broadcast_to(wts.T[:, :, None], (n, TOP_K, BLK))
    out = _combine(pos3, h1, wts3, ys, w_sh_gate[0].astype(BF16), w_sh_up[0].astype(BF16),
                   w_sh_down[0].astype(BF16), row2(ln_ffn_g[0]), row2(ln_ffn_b[0]), 128)
    return out.reshape(batch, seq, d)
```

```python
import functools

import jax
import jax.numpy as jnp
from jax import lax
from jax.experimental import pallas as pl
from jax.experimental.pallas import tpu as pltpu

F32 = jnp.float32
BF16 = jnp.bfloat16
I32 = jnp.int32

D_MODEL = 2048
N_META = 16
BLK = 128
A_HEADS = 16
A_KV_HEADS = 4
A_HEAD_DIM = 64
A_WINDOW = 128
B_HEADS = 8
B_Q_RANK = 512
B_KV_RANK = 256
B_NOPE = 128
B_ROPE = 64
B_V = 128
ROPE_THETA = 10000.0
A_Q_W = A_HEADS * A_HEAD_DIM
A_KV_W = A_KV_HEADS * A_HEAD_DIM
N_EXPERTS = 256
N_GROUPS = 8
GROUP_SIZE = N_EXPERTS // N_GROUPS
TOPK_GROUPS = 4
TOP_K = 8
D_EXPERT = 512
D_SHARED = 512
ROUTED_SCALE = 2.5
MOE_BLOCK = 256
DEPTH = 1
DN_ALPHA = (2 * DEPTH) ** 0.25
LN_EPS = 1e-5
RMS_EPS = 1e-6
NEG_INF = -1e30

A_SCALE = A_HEAD_DIM ** -0.5
B_SCALE = (B_NOPE + B_ROPE) ** -0.5
B_QK = B_NOPE + B_ROPE

SUBLANES = 8
LANES = 128
ROW_TILE = (D_MODEL // LANES, LANES)
assert ROW_TILE[0] == 2 * SUBLANES

VMEM_LIMIT = 56 * 1024 * 1024

NT_DIMS = (((1,), (1,)), ((), ()))


def _const_spec(shape):
    zeros = (0,) * len(shape)
    return pl.BlockSpec(shape, lambda *_: zeros, pipeline_mode=pl.Buffered(1))


def _layer_norm(z, g, b):
    mu = jnp.mean(z, axis=-1, keepdims=True)
    zc = z - mu
    var = jnp.mean(zc * zc, axis=-1, keepdims=True)
    return zc * lax.rsqrt(var + LN_EPS) * g + b


def _rms_norm(z, g):
    return z * lax.rsqrt(jnp.mean(z * z, axis=-1, keepdims=True) + RMS_EPS) * g


def _silu(z):
    return z * jax.nn.sigmoid(z)


def _embed_proj_kernel(x_ref, g_ref, b_ref, win_ref, gq_ref, wq_ref, gkv_ref, wkv_ref, cos_ref, sin_ref, rep_ref,
                       h0_ref, qa_ref, ka_ref, va_ref, qm_ref, km_ref, vm_ref):
    h = _layer_norm(x_ref[...], g_ref[...], b_ref[...])
    hb = h.astype(BF16)
    h0_ref[...] = hb
    p = jnp.dot(hb, win_ref[...], preferred_element_type=F32)
    o = 0
    qa_ref[...] = (p[:, o:o + A_Q_W] * A_SCALE).astype(BF16)
    o += A_Q_W
    ka_ref[...] = jnp.dot(p[:, o:o + A_KV_W].astype(BF16), rep_ref[...], preferred_element_type=F32).astype(BF16)
    o += A_KV_W
    va_ref[...] = jnp.dot(p[:, o:o + A_KV_W].astype(BF16), rep_ref[...], preferred_element_type=F32).astype(BF16)
    o += A_KV_W
    cq = p[:, o:o + B_Q_RANK]
    o += B_Q_RANK
    ckv = p[:, o:o + B_KV_RANK]
    o += B_KV_RANK
    kpe = p[:, o:o + B_ROPE]
    o += B_ROPE
    kpe_rot = p[:, o:o + B_ROPE]

    cos = cos_ref[...]
    sin = sin_ref[...]
    qall = jnp.dot(_rms_norm(cq, gq_ref[...]).astype(BF16), wq_ref[...], preferred_element_type=F32)
    n_nope = B_HEADS * B_NOPE
    n_pe = B_HEADS * B_ROPE
    q_pe = qall[:, n_nope:n_nope + n_pe] * cos + qall[:, n_nope + n_pe:] * sin
    kv = jnp.dot(_rms_norm(ckv, gkv_ref[...]).astype(BF16), wkv_ref[...], preferred_element_type=F32)
    k_pe = (kpe * cos[:, :B_ROPE] + kpe_rot * sin[:, :B_ROPE]).astype(BF16)
    for hd in range(B_HEADS):
        qn = (qall[:, hd * B_NOPE:(hd + 1) * B_NOPE] * B_SCALE).astype(BF16)
        qp = (q_pe[:, hd * B_ROPE:(hd + 1) * B_ROPE] * B_SCALE).astype(BF16)
        qm_ref[hd] = jnp.concatenate([qn, qp], axis=1)
        kn = kv[:, hd * B_NOPE:(hd + 1) * B_NOPE].astype(BF16)
        km_ref[hd] = jnp.concatenate([kn, k_pe], axis=1)
        vm_ref[hd] = kv[:, n_nope + hd * B_V:n_nope + (hd + 1) * B_V].astype(BF16)


def _embed_proj(x2, batch, seq, tm, ln_g, ln_b, w_in, gq, wq, gkv, wkv, rep, cos, sin):
    n = batch * seq
    nt = seq // tm
    row = lambda b, i: (b * nt + i, 0)
    head_major = lambda b, i: (b, 0, i, 0)
    in_w = w_in.shape[1]
    return pl.pallas_call(
        _embed_proj_kernel,
        grid=(batch, nt),
        in_specs=[
            pl.BlockSpec((tm, D_MODEL), row),
            _const_spec((1, D_MODEL)), _const_spec((1, D_MODEL)),
            _const_spec((D_MODEL, in_w)),
            _const_spec((1, B_Q_RANK)), _const_spec(wq.shape),
            _const_spec((1, B_KV_RANK)), _const_spec(wkv.shape),
            pl.BlockSpec((tm, B_HEADS * B_ROPE), lambda b, i: (i, 0)),
            pl.BlockSpec((tm, B_HEADS * B_ROPE), lambda b, i: (i, 0)),
            _const_spec(rep.shape),
        ],
        out_specs=[
            pl.BlockSpec((tm, D_MODEL), row),
            pl.BlockSpec((tm, A_Q_W), row),
            pl.BlockSpec((tm, A_Q_W), row),
            pl.BlockSpec((tm, A_Q_W), row),
            pl.BlockSpec((None, B_HEADS, tm, B_QK), head_major),
            pl.BlockSpec((None, B_HEADS, tm, B_QK), head_major),
            pl.BlockSpec((None, B_HEADS, tm, B_V), head_major),
        ],
        out_shape=[
            jax.ShapeDtypeStruct((n, D_MODEL), BF16),
            jax.ShapeDtypeStruct((n, A_Q_W), BF16),
            jax.ShapeDtypeStruct((n, A_Q_W), BF16),
            jax.ShapeDtypeStruct((n, A_Q_W), BF16),
            jax.ShapeDtypeStruct((batch, B_HEADS, seq, B_QK), BF16),
            jax.ShapeDtypeStruct((batch, B_HEADS, seq, B_QK), BF16),
            jax.ShapeDtypeStruct((batch, B_HEADS, seq, B_V), BF16),
        ],
        compiler_params=pltpu.CompilerParams(
            dimension_semantics=("arbitrary", "arbitrary"), vmem_limit_bytes=VMEM_LIMIT),
        name="embed_proj",
    )(x2, ln_g, ln_b, w_in, gq, wq, gkv, wkv, cos, sin, rep)


def _win_attn_kernel(sink_ref, q_ref, kp_ref, kc_ref, kn_ref, vp_ref, vc_ref, vn_ref, kmeta_ref, vmeta_ref,
                     bias_ref, o_ref):
    first = pl.program_id(1) == 0
    grp = A_HEADS // A_KV_HEADS
    gw = grp * A_HEAD_DIM
    k3 = jnp.concatenate([jnp.where(first, kmeta_ref[...], kp_ref[...]), kc_ref[...], kn_ref[...]], axis=0)
    v3 = jnp.concatenate([jnp.where(first, vmeta_ref[...], vp_ref[...]), vc_ref[...], vn_ref[...]], axis=0)
    lane_head = lax.broadcasted_iota(I32, (BLK, gw), 1) // A_HEAD_DIM
    row_head = lax.broadcasted_iota(I32, (grp * BLK, 1), 0) // BLK
    for kvh in range(A_KV_HEADS):
        cols = slice(kvh * gw, (kvh + 1) * gw)
        qg = q_ref[:, cols]
        q4 = jnp.concatenate([jnp.where(lane_head == r, qg, jnp.zeros_like(qg)) for r in range(grp)], axis=0)
        sink = jnp.zeros((grp * BLK, 1), F32)
        for r in range(grp):
            sink = jnp.where(row_head == r, sink_ref[kvh * grp + r], sink)
        s = lax.dot_general(q4, k3[:, cols], NT_DIMS, preferred_element_type=F32) + bias_ref[kvh]
        m = jnp.maximum(jnp.max(s, axis=-1, keepdims=True), sink)
        p = jnp.exp(s - m)
        denom = jnp.sum(p, axis=-1, keepdims=True) + jnp.exp(sink - m)
        ow = jnp.dot(p.astype(BF16), v3[:, cols], preferred_element_type=F32) / denom
        og = jnp.zeros((BLK, gw), F32)
        for r in range(grp):
            og = jnp.where(lane_head == r, ow[r * BLK:(r + 1) * BLK], og)
        o_ref[:, cols] = og.astype(BF16)


def _win_bias():
    qi = jnp.arange(BLK)[:, None]
    ki = jnp.arange(3 * BLK)[None, :]
    dist = jnp.abs(BLK + qi - ki)
    in_window = dist <= A_WINDOW
    valid = jnp.stack([in_window, in_window & (ki >= BLK - N_META), in_window & (ki < 2 * BLK)])
    slopes = jnp.exp2(-8.0 * jnp.arange(1, A_HEADS + 1, dtype=F32) / A_HEADS)
    bias = jnp.where(valid[:, None], -slopes[None, :, None, None] * dist.astype(F32)[None, None], NEG_INF)
    return bias.reshape(3, A_KV_HEADS, (A_HEADS // A_KV_HEADS) * BLK, 3 * BLK)


def _win_attn(sink, qa, ka, va, kmeta, vmeta, batch, seq):
    nb = seq // BLK
    n = batch * seq
    bias = _win_bias()
    cur = lambda b, i, *_: (b * nb + i, 0)
    prev = lambda b, i, *_: (b * nb + jnp.maximum(i - 1, 0), 0)
    nxt = lambda b, i, *_: (b * nb + jnp.minimum(i + 1, nb - 1), 0)
    kv_spec = lambda f: pl.BlockSpec((BLK, A_Q_W), f)
    const = lambda b, i, *_: (0, 0)
    variant = lambda b, i, *_: (jnp.where(i == 0, 1, jnp.where(i == nb - 1, 2, 0)), 0, 0, 0)
    return pl.pallas_call(
        _win_attn_kernel,
        grid_spec=pltpu.PrefetchScalarGridSpec(
            num_scalar_prefetch=1,
            grid=(batch, nb),
            in_specs=[
                pl.BlockSpec((BLK, A_Q_W), cur),
                kv_spec(prev), kv_spec(cur), kv_spec(nxt),
                kv_spec(prev), kv_spec(cur), kv_spec(nxt),
                pl.BlockSpec((BLK, A_Q_W), const), pl.BlockSpec((BLK, A_Q_W), const),
                pl.BlockSpec((None,) + bias.shape[1:], variant),
            ],
            out_specs=pl.BlockSpec((BLK, A_Q_W), cur),
        ),
        out_shape=jax.ShapeDtypeStruct((n, A_Q_W), BF16),
        compiler_params=pltpu.CompilerParams(
            dimension_semantics=("arbitrary", "arbitrary"), vmem_limit_bytes=VMEM_LIMIT),
        name="win_attn",
    )(sink, qa, ka, ka, ka, va, va, va, kmeta, vmeta, bias)


MLA_KEY_CHUNKS = 2


def _mla_attn_kernel(q_ref, k_ref, v_ref, km_ref, vm_ref, o_ref):
    q = q_ref[...]
    sm = lax.dot_general(q, km_ref[...], NT_DIMS, preferred_element_type=F32)
    m = jnp.max(sm, axis=-1, keepdims=True)
    pm = jnp.exp(sm - m)
    denom = jnp.sum(pm, axis=-1, keepdims=True)
    acc = jnp.dot(pm.astype(BF16), vm_ref[...], preferred_element_type=F32)
    chunk = k_ref.shape[0] // MLA_KEY_CHUNKS
    for c in range(MLA_KEY_CHUNKS):
        keys = slice(c * chunk, (c + 1) * chunk)
        s = lax.dot_general(q, k_ref[keys, :], NT_DIMS, preferred_element_type=F32)
        m_new = jnp.maximum(m, jnp.max(s, axis=-1, keepdims=True))
        rescale = jnp.exp(m - m_new)
        p = jnp.exp(s - m_new)
        denom = rescale * denom + jnp.sum(p, axis=-1, keepdims=True)
        acc = rescale * acc + jnp.dot(p.astype(BF16), v_ref[keys, :], preferred_element_type=F32)
        m = m_new
    o_ref[...] = (acc / denom).astype(BF16)


def _mla_attn(qm, km, vm, km_meta, vm_meta, tq):
    batch, heads, seq, _ = qm.shape
    nq = seq // tq
    return pl.pallas_call(
        _mla_attn_kernel,
        grid=(batch, heads, nq),
        in_specs=[
            pl.BlockSpec((None, None, tq, B_QK), lambda b, h, i: (b, h, i, 0)),
            pl.BlockSpec((None, None, seq, B_QK), lambda b, h, i: (b, h, 0, 0)),
            pl.BlockSpec((None, None, seq, B_V), lambda b, h, i: (b, h, 0, 0)),
            pl.BlockSpec((None, None, N_META, B_QK), lambda b, h, i: (0, h, 0, 0)),
            pl.BlockSpec((None, None, N_META, B_V), lambda b, h, i: (0, h, 0, 0)),
        ],
        out_specs=pl.BlockSpec((tq, B_V), lambda b, h, i: (b * nq + i, h)),
        out_shape=jax.ShapeDtypeStruct((batch * seq, heads * B_V), BF16),
        compiler_params=pltpu.CompilerParams(
            dimension_semantics=("arbitrary", "arbitrary", "arbitrary"), vmem_limit_bytes=VMEM_LIMIT),
        name="mla_attn",
    )(qm, km, vm, km_meta, vm_meta)


def _out_proj_kernel(oa_ref, ob_ref, h0_ref, wa_ref, wb_ref, g_ref, b_ref, wr_hi_ref, wr_lo_ref,
                     h1_ref, lg_ref):
    mix = jnp.dot(oa_ref[...], wa_ref[...], preferred_element_type=F32)
    mix = mix + jnp.dot(ob_ref[...], wb_ref[...], preferred_element_type=F32)
    h1 = _layer_norm(DN_ALPHA * h0_ref[...].astype(F32) + mix, g_ref[...], b_ref[...])
    hi = h1.astype(BF16)
    hi_f = hi.astype(F32)
    lo = (h1 - hi_f).astype(BF16)
    lg = lax.dot_general(wr_hi_ref[...], hi, NT_DIMS, preferred_element_type=F32)
    lg = lg + lax.dot_general(wr_hi_ref[...], lo, NT_DIMS, preferred_element_type=F32)
    lg = lg + lax.dot_general(wr_lo_ref[...], hi, NT_DIMS, preferred_element_type=F32)
    lg_ref[...] = lg
    h1_ref[...] = hi.reshape(h1_ref.shape)


def _out_proj(oa, ob, h0, wa, wb, g, b, wr_hi, wr_lo, tm):
    n = oa.shape[0]
    row = lambda i: (i, 0)
    return pl.pallas_call(
        _out_proj_kernel,
        grid=(n // tm,),
        in_specs=[
            pl.BlockSpec((tm, A_Q_W), row),
            pl.BlockSpec((tm, B_HEADS * B_V), row),
            pl.BlockSpec((tm, D_MODEL), row),
            _const_spec(wa.shape), _const_spec(wb.shape),
            _const_spec((1, D_MODEL)), _const_spec((1, D_MODEL)),
            _const_spec(wr_hi.shape), _const_spec(wr_lo.shape),
        ],
        out_specs=[
            pl.BlockSpec((tm,) + ROW_TILE, lambda i: (i, 0, 0)),
            pl.BlockSpec((N_EXPERTS, tm), lambda i: (0, i)),
        ],
        out_shape=[
            jax.ShapeDtypeStruct((n,) + ROW_TILE, BF16),
            jax.ShapeDtypeStruct((N_EXPERTS, n), F32),
        ],
        compiler_params=pltpu.CompilerParams(
            dimension_semantics=("arbitrary",), vmem_limit_bytes=VMEM_LIMIT),
        name="out_proj",
    )(oa, ob, h0, wa, wb, g, b, wr_hi, wr_lo)


def _first_max(cur, idx, sentinel):
    m = jnp.max(cur, axis=0, keepdims=True)
    am = jnp.min(jnp.where(cur == m, idx, sentinel), axis=0, keepdims=True)
    return m, am


def _route_kernel(lg_ref, bias_ref, eidx_ref, wts_ref, rank_ref, cnt_ref, carry_ref):
    @pl.when(pl.program_id(0) == 0)
    def _():
        carry_ref[...] = jnp.zeros_like(carry_ref)

    t = lg_ref.shape[1]
    scores = jax.nn.sigmoid(lg_ref[...])
    sel = scores + bias_ref[...]
    neg_inf = jnp.float32(-jnp.inf)

    r32 = lax.broadcasted_iota(I32, (GROUP_SIZE, t), 0)
    gs = []
    for g in range(N_GROUPS):
        blk = sel[g * GROUP_SIZE:(g + 1) * GROUP_SIZE]
        m1, i1 = _first_max(blk, r32, GROUP_SIZE)
        m2 = jnp.max(jnp.where(r32 == i1, neg_inf, blk), axis=0, keepdims=True)
        gs.append(m1 + m2)
    cur = jnp.concatenate(gs, axis=0)
    r8 = lax.broadcasted_iota(I32, (N_GROUPS, t), 0)
    keep = r8 < 0
    for _ in range(TOPK_GROUPS):
        _, ig = _first_max(cur, r8, N_GROUPS)
        hit = r8 == ig
        keep = keep | hit
        cur = jnp.where(hit, neg_inf, cur)
    cur = jnp.concatenate(
        [jnp.where(keep[g:g + 1], sel[g * GROUP_SIZE:(g + 1) * GROUP_SIZE], NEG_INF) for g in range(N_GROUPS)],
        axis=0)

    row = lax.broadcasted_iota(I32, (N_EXPERTS, t), 0)
    chosen = jnp.zeros((N_EXPERTS, t), F32)
    idxs, ws = [], []
    for _ in range(TOP_K):
        _, ik = _first_max(cur, row, N_EXPERTS)
        hit = row == ik
        ws.append(jnp.sum(jnp.where(hit, scores, 0.0), axis=0, keepdims=True))
        cur = jnp.where(hit, neg_inf, cur)
        chosen = jnp.where(hit, 1.0, chosen)
        idxs.append(ik)
    eidx = jnp.concatenate(idxs, axis=0)
    w = jnp.concatenate(ws, axis=0)
    eidx_ref[...] = eidx
    wts_ref[...] = w / jnp.sum(w, axis=0, keepdims=True) * ROUTED_SCALE

    before = lax.broadcasted_iota(I32, (t, t), 0) < lax.broadcasted_iota(I32, (t, t), 1)
    prefix = jnp.dot(chosen.astype(BF16), jnp.where(before, 1.0, 0.0).astype(BF16), preferred_element_type=F32)
    prefix = prefix + carry_ref[...]
    ranks = [jnp.sum(jnp.where(row == idxs[k], prefix, 0.0), axis=0, keepdims=True) for k in range(TOP_K)]
    rank_ref[...] = jnp.concatenate(ranks, axis=0).astype(I32)
    carry_ref[...] = carry_ref[...] + jnp.sum(chosen, axis=1, keepdims=True)
    cnt_ref[...] = carry_ref[...].astype(I32)


def _route(logits_t, bias, t):
    n = logits_t.shape[1]
    col = lambda i: (0, i)
    return pl.pallas_call(
        _route_kernel,
        grid=(n // t,),
        in_specs=[pl.BlockSpec((N_EXPERTS, t), col), _const_spec((N_EXPERTS, 1))],
        out_specs=[
            pl.BlockSpec((TOP_K, t), col), pl.BlockSpec((TOP_K, t), col), pl.BlockSpec((TOP_K, t), col),
            pl.BlockSpec((N_EXPERTS, 1), lambda i: (0, 0)),
        ],
        out_shape=[
            jax.ShapeDtypeStruct((TOP_K, n), I32),
            jax.ShapeDtypeStruct((TOP_K, n), F32),
            jax.ShapeDtypeStruct((TOP_K, n), I32),
            jax.ShapeDtypeStruct((N_EXPERTS, 1), I32),
        ],
        scratch_shapes=[pltpu.VMEM((N_EXPERTS, 1), F32)],
        compiler_params=pltpu.CompilerParams(
            dimension_semantics=("arbitrary",), vmem_limit_bytes=VMEM_LIMIT),
        name="route",
    )(logits_t, bias)


def _positions_kernel(eidx_ref, rank_ref, start_ref, pos_ref):
    t = eidx_ref.shape[1]
    row = lax.broadcasted_iota(I32, (N_EXPERTS, t), 0)
    start = start_ref[...]
    eidx = eidx_ref[...]
    base = [jnp.sum(jnp.where(row == eidx[k:k + 1], start, 0), axis=0, keepdims=True) for k in range(TOP_K)]
    pos_ref[...] = jnp.concatenate(base, axis=0) + rank_ref[...]


def _positions(eidx, rank, starts, t):
    n = eidx.shape[1]
    col = lambda i: (0, i)
    return pl.pallas_call(
        _positions_kernel,
        grid=(n // t,),
        in_specs=[pl.BlockSpec((TOP_K, t), col), pl.BlockSpec((TOP_K, t), col), _const_spec((N_EXPERTS, 1))],
        out_specs=pl.BlockSpec((None, TOP_K, t), lambda i: (i, 0, 0)),
        out_shape=jax.ShapeDtypeStruct((n // t, TOP_K, t), I32),
        compiler_params=pltpu.CompilerParams(dimension_semantics=("arbitrary",)),
        name="positions",
    )(eidx, rank, starts)


def _zero_fill(row0_ref, len_ref, nu_ref, zbuf, xs_ref, sem, wait):
    def run(copy):
        copy.wait() if wait else copy.start()

    def per_expert(e, carry):
        r0 = row0_ref[e]
        ln = len_ref[e]
        bit = MOE_BLOCK // 2
        while bit >= 1:
            off = ln - (ln & (2 * bit - 1))

            @pl.when((ln & bit) != 0)
            def _(bit=bit, off=off):
                run(pltpu.make_async_copy(zbuf.at[pl.ds(0, bit)], xs_ref.at[pl.ds(r0 + off, bit)], sem))
            bit //= 2
        return carry

    lax.fori_loop(0, N_EXPERTS, per_expert, 0)

    def per_block(g, carry):
        run(pltpu.make_async_copy(zbuf, xs_ref.at[pl.ds(g * MOE_BLOCK, MOE_BLOCK)], sem))
        return carry

    lax.fori_loop(nu_ref[0], xs_ref.shape[0] // MOE_BLOCK, per_block, 0)


def _dispatch_kernel(row0_ref, len_ref, nu_ref, pos_ref, h_ref, xs_ref, zbuf, sem, zsem):
    td = h_ref.shape[0]

    @pl.when(pl.program_id(0) == 0)
    def _():
        zbuf[...] = jnp.zeros_like(zbuf)
        _zero_fill(row0_ref, len_ref, nu_ref, zbuf, xs_ref, zsem, wait=False)

    def issue(t, carry):
        for k in range(TOP_K):
            pltpu.make_async_copy(h_ref.at[t], xs_ref.at[pos_ref[k, t]], sem).start(priority=k % 2)
        return carry

    lax.fori_loop(0, td, issue, 0, unroll=4)
    for _ in range(TOP_K):
        pltpu.make_async_copy(h_ref, xs_ref.at[pl.ds(0, td)], sem).wait()

    @pl.when(pl.program_id(0) == pl.num_programs(0) - 1)
    def _():
        _zero_fill(row0_ref, len_ref, nu_ref, zbuf, xs_ref, zsem, wait=True)


def _dispatch(pad_row0, pad_len, n_used, pos3, h1, n_rows, td):
    n = h1.shape[0]
    tp = pos3.shape[2]
    per = tp // td
    return pl.pallas_call(
        _dispatch_kernel,
        grid_spec=pltpu.PrefetchScalarGridSpec(
            num_scalar_prefetch=3,
            grid=(n // td,),
            in_specs=[
                pl.BlockSpec((None, TOP_K, td), lambda i, *_: (i // per, 0, i % per), memory_space=pltpu.SMEM),
                pl.BlockSpec((td,) + ROW_TILE, lambda i, *_: (i, 0, 0)),
            ],
            out_specs=pl.BlockSpec(memory_space=pl.ANY),
            scratch_shapes=[pltpu.VMEM((MOE_BLOCK,) + ROW_TILE, BF16), pltpu.SemaphoreType.DMA,
                            pltpu.SemaphoreType.DMA],
        ),
        out_shape=jax.ShapeDtypeStruct((n_rows,) + ROW_TILE, BF16),
        compiler_params=pltpu.CompilerParams(dimension_semantics=("arbitrary",)),
        name="dispatch",
    )(pad_row0, pad_len, n_used, pos3, h1)


CAST_CHUNK_ELEMS = 32 * 1024


def _cast_rows(src, dst):
    chunk = CAST_CHUNK_ELEMS // src.shape[1]
    for r in range(0, src.shape[0], chunk):
        dst[r:r + chunk, :] = src[r:r + chunk, :].astype(dst.dtype)


WEIGHT_DMA_PRIORITY = 1
X_AHEAD = 2


def _moe_kernel(be_ref, nu_ref, nxt_ref, slot_ref, xs_hbm, wg_hbm, wu_hbm, wd_hbm, ys_hbm,
                xbuf, ybuf, wg_land, wu_land, wd_land, wg_s, wu_s, wd_s, xsem, ysem, wsems):
    n_used = nu_ref[0]
    n_blocks = xs_hbm.shape[0] // MOE_BLOCK

    def rows(b):
        return pl.ds(b * MOE_BLOCK, MOE_BLOCK)

    def x_copy(b, s):
        return pltpu.make_async_copy(xs_hbm.at[rows(b)], xbuf.at[s], xsem.at[s])

    def y_copy(b, s):
        return pltpu.make_async_copy(ybuf.at[s], ys_hbm.at[rows(b)], ysem.at[s])

    def weight_copies(e, s):
        return (pltpu.make_async_copy(wg_hbm.at[e], wg_land.at[s], wsems.at[s, 0]),
                pltpu.make_async_copy(wu_hbm.at[e], wu_land.at[s], wsems.at[s, 1]),
                pltpu.make_async_copy(wd_hbm.at[e], wd_land.at[s], wsems.at[s, 2]))

    for j in range(X_AHEAD):
        @pl.when(j < n_used)
        def _(j=j):
            x_copy(j, j).start()

    for c in weight_copies(be_ref[0], slot_ref[0]):
        c.start(priority=WEIGHT_DMA_PRIORITY)

    def block(b, carry):
        s = b % 2
        expert = be_ref[b]
        slot = slot_ref[b]
        new_expert = (b == 0) | (expert != be_ref[jnp.maximum(b - 1, 0)])

        xslot = b % (X_AHEAD + 1)

        @pl.when(b + X_AHEAD < n_used)
        def _():
            x_copy(b + X_AHEAD, (b + X_AHEAD) % (X_AHEAD + 1)).start()

        x_copy(b, xslot).wait()

        @pl.when(b >= 2)
        def _():
            y_copy(b - 2, s).wait()

        def expert_mlp():
            x = xbuf[xslot].reshape(MOE_BLOCK, D_MODEL)
            gate = jnp.dot(x, wg_s[...], preferred_element_type=F32)
            up = jnp.dot(x, wu_s[...], preferred_element_type=F32)
            act = (_silu(gate) * up).astype(BF16)
            y = jnp.dot(act, wd_s[...], preferred_element_type=F32)
            ybuf[s] = y.astype(BF16).reshape(ybuf.shape[1:])

        @pl.when(new_expert)
        def _():
            nxt = nxt_ref[b]

            @pl.when(nxt >= 0)
            def _():
                for c in weight_copies(nxt, 1 - slot):
                    c.start(priority=WEIGHT_DMA_PRIORITY)

            for c in weight_copies(expert, slot):
                c.wait()
            _cast_rows(wg_land.at[slot], wg_s)
            _cast_rows(wu_land.at[slot], wu_s)
            _cast_rows(wd_land.at[slot], wd_s)
            expert_mlp()

        @pl.when(jnp.logical_not(new_expert))
        def _():
            expert_mlp()

        y_copy(b, s).start()
        return carry

    lax.fori_loop(0, n_used, block, 0)

    @pl.when(n_used >= 2)
    def _():
        y_copy(n_used - 2, n_used % 2).wait()

    y_copy(n_used - 1, (n_used - 1) % 2).wait()

    ybuf[0] = jnp.zeros(ybuf.shape[1:], ybuf.dtype)

    def fill(b, carry):
        y_copy(b, 0).start()
        return carry

    def fill_done(b, carry):
        y_copy(b, 0).wait()
        return carry

    lax.fori_loop(n_used, n_blocks, fill, 0)
    lax.fori_loop(n_used, n_blocks, fill_done, 0)


def _moe(block_e, n_used, next_e, slot, xs, w_gate, w_up, w_down):
    hbm = pl.BlockSpec(memory_space=pl.ANY)
    return pl.pallas_call(
        _moe_kernel,
        grid_spec=pltpu.PrefetchScalarGridSpec(
            num_scalar_prefetch=4,
            grid=(1,),
            in_specs=[hbm, hbm, hbm, hbm],
            out_specs=hbm,
            scratch_shapes=[
                pltpu.VMEM((X_AHEAD + 1, MOE_BLOCK) + ROW_TILE, BF16),
                pltpu.VMEM((2, MOE_BLOCK) + ROW_TILE, BF16),
                pltpu.VMEM((2, D_MODEL, D_EXPERT), F32),
                pltpu.VMEM((2, D_MODEL, D_EXPERT), F32),
                pltpu.VMEM((2, D_EXPERT, D_MODEL), F32),
                pltpu.VMEM((D_MODEL, D_EXPERT), BF16),
                pltpu.VMEM((D_MODEL, D_EXPERT), BF16),
                pltpu.VMEM((D_EXPERT, D_MODEL), BF16),
                pltpu.SemaphoreType.DMA((X_AHEAD + 1,)),
                pltpu.SemaphoreType.DMA((2,)),
                pltpu.SemaphoreType.DMA((2, 3)),
            ],
        ),
        out_shape=jax.ShapeDtypeStruct(xs.shape, xs.dtype),
        compiler_params=pltpu.CompilerParams(
            dimension_semantics=("arbitrary",), vmem_limit_bytes=VMEM_LIMIT),
        name="moe",
    )(block_e, n_used, next_e, slot, xs, w_gate, w_up, w_down)


def _combine_kernel(pos_ref, nxt_ref, h_ref, wts_ref, ys_ref, wsg_ref, wsu_ref, wsd_ref, g_ref, b_ref,
                    o_ref, *scratch):
    gbufs, (acc_ref, sems) = scratch[:COMBINE_TILES], scratch[COMBINE_TILES:]
    tc = h_ref.shape[0] // COMBINE_TILES
    step = pl.program_id(0)

    def issue(pos, t0, gbuf, sem):
        for t in range(tc):
            for k in range(TOP_K):
                pltpu.make_async_copy(ys_ref.at[pos[k, t0 + t]], gbuf.at[k, t], sem).start(priority=k % 2)

    def drain(gbuf, sem):
        for k in range(TOP_K):
            pltpu.make_async_copy(ys_ref.at[pl.ds(0, tc)], gbuf.at[k], sem).wait()

    def shared_expert(rows):
        hb = h_ref[rows].reshape(tc, D_MODEL)
        act = _silu(jnp.dot(hb, wsg_ref[...], preferred_element_type=F32))
        act = (act * jnp.dot(hb, wsu_ref[...], preferred_element_type=F32)).astype(BF16)
        return DN_ALPHA * hb.astype(F32) + jnp.dot(act, wsd_ref[...], preferred_element_type=F32)

    def finish(rows, base, gbuf):
        for r in range(0, tc, SUBLANES):
            wts = wts_ref[rows.start + r:rows.start + r + SUBLANES]
            acc = jnp.zeros((SUBLANES,) + ROW_TILE, F32)
            for k in range(TOP_K):
                wk = jnp.broadcast_to(wts[:, k:k + 1, :], acc.shape)
                acc = acc + wk * gbuf[k, r:r + SUBLANES].astype(F32)
            acc_ref[r:r + SUBLANES, :] = acc.reshape(SUBLANES, D_MODEL)
        o_ref[rows, :] = _layer_norm(base + acc_ref[...], g_ref[...], b_ref[...])

    @pl.when(step == 0)
    def _():
        for j in range(COMBINE_AHEAD):
            issue(pos_ref, j * tc, gbufs[j], sems.at[j])

    for j in range(COMBINE_TILES):
        rows = slice(j * tc, (j + 1) * tc)
        drain(gbufs[j], sems.at[j])
        finish(rows, shared_expert(rows), gbufs[j])
        ahead = j + COMBINE_AHEAD
        if ahead < COMBINE_TILES:
            issue(pos_ref, ahead * tc, gbufs[ahead], sems.at[ahead])
        else:
            issue(nxt_ref, (ahead - COMBINE_TILES) * tc, gbufs[ahead - COMBINE_TILES],
                  sems.at[ahead - COMBINE_TILES])

    @pl.when(step == pl.num_programs(0) - 1)
    def _():
        for j in range(COMBINE_AHEAD):
            drain(gbufs[j], sems.at[j])


COMBINE_TILES = 4
COMBINE_AHEAD = 2


def _combine(pos3, h1, wts3, ys, wsg, wsu, wsd, g, b, tc):
    n = h1.shape[0]
    rows = COMBINE_TILES * tc
    assert pos3.shape[2] == rows and n % rows == 0
    steps = n // rows
    row3 = lambda i: (i, 0, 0)
    return pl.pallas_call(
        _combine_kernel,
        grid=(steps,),
        in_specs=[
            pl.BlockSpec((None, TOP_K, rows), row3, memory_space=pltpu.SMEM),
            pl.BlockSpec((None, TOP_K, COMBINE_AHEAD * tc), lambda i: (jnp.minimum(i + 1, steps - 1), 0, 0),
                         memory_space=pltpu.SMEM),
            pl.BlockSpec((rows,) + ROW_TILE, row3),
            pl.BlockSpec((rows,) + wts3.shape[1:], row3),
            pl.BlockSpec(memory_space=pl.ANY),
            _const_spec(wsg.shape), _const_spec(wsu.shape), _const_spec(wsd.shape),
            _const_spec((1, D_MODEL)), _const_spec((1, D_MODEL)),
        ],
        out_specs=pl.BlockSpec((rows, D_MODEL), lambda i: (i, 0)),
        out_shape=jax.ShapeDtypeStruct((n, D_MODEL), F32),
        scratch_shapes=[pltpu.VMEM((TOP_K, tc) + ROW_TILE, BF16)] * COMBINE_TILES + [
            pltpu.VMEM((tc, D_MODEL), F32), pltpu.SemaphoreType.DMA((COMBINE_TILES,))],
        compiler_params=pltpu.CompilerParams(
            dimension_semantics=("arbitrary",), vmem_limit_bytes=VMEM_LIMIT),
        name="combine",
    )(pos3, pos3, h1, wts3, ys, wsg, wsu, wsd, g, b)


def _rotate_half_cols(w):
    half = w.shape[-1] // 2
    return jnp.concatenate([-w[..., half:], w[..., :half]], axis=-1)


def _rope_tables(pos):
    inv_freq = ROPE_THETA ** (-jnp.arange(0, B_ROPE, 2, dtype=F32) / B_ROPE)
    ang = pos[:, None] * inv_freq[None, :]
    cos = jnp.tile(jnp.cos(ang), (1, 2 * B_HEADS))
    sin = jnp.tile(jnp.sin(ang), (1, 2 * B_HEADS))
    return cos, sin


def kernel(x, meta_tokens, ln_emb_g, ln_emb_b, w_in, q_norm_g, w_q_up, kv_norm_g, w_kv_up, sink_logits, w_out,
           ln_mix_g, ln_mix_b, w_router, router_bias, w_exp_gate, w_exp_up, w_exp_down, w_sh_gate, w_sh_up,
           w_sh_down, ln_ffn_g, ln_ffn_b):
    batch, seq, d = x.shape
    assert d == D_MODEL and seq % 512 == 0 and w_in.shape[0] == DEPTH
    n = batch * seq
    row2 = lambda v: v.reshape(1, -1).astype(F32)

    w_in0 = w_in[0]
    kpe_cols = w_in0[:, -B_ROPE:]
    w_in_b = jnp.concatenate([w_in0, _rotate_half_cols(kpe_cols)], axis=1).astype(BF16)
    wq = w_q_up[0]
    wq_pe = wq[:, :, B_NOPE:]
    wq_b = jnp.concatenate([
        wq[:, :, :B_NOPE].reshape(B_Q_RANK, -1),
        wq_pe.reshape(B_Q_RANK, -1),
        _rotate_half_cols(wq_pe).reshape(B_Q_RANK, -1)], axis=1).astype(BF16)
    wkv = w_kv_up[0]
    wkv_b = jnp.concatenate([
        wkv[:, :, :B_NOPE].reshape(B_KV_RANK, -1), wkv[:, :, B_NOPE:].reshape(B_KV_RANK, -1)], axis=1).astype(BF16)
    w_out_b = w_out[0].astype(BF16)
    wr_t = w_router[0].astype(F32).T
    wr_hi = wr_t.astype(BF16)
    wr_lo = (wr_t - wr_hi.astype(F32)).astype(BF16)

    cos_m, sin_m = _rope_tables(jnp.arange(N_META, dtype=F32))
    cos_x, sin_x = _rope_tables(jnp.arange(N_META, N_META + seq, dtype=F32))
    rep_col = jnp.arange(A_Q_W)
    rep_src = rep_col // (A_Q_W // A_KV_HEADS) * A_HEAD_DIM + rep_col % A_HEAD_DIM
    rep = (jnp.arange(A_KV_W)[:, None] == rep_src[None, :]).astype(BF16)
    proj_args = (row2(ln_emb_g), row2(ln_emb_b), w_in_b, row2(q_norm_g[0]), wq_b, row2(kv_norm_g[0]), wkv_b, rep)
    _, _, ka_m, va_m, _, km_m, vm_m = _embed_proj(
        meta_tokens.astype(F32), 1, N_META, N_META, *proj_args, cos_m, sin_m)
    h0, qa, ka, va, qm, km, vm = _embed_proj(x.reshape(n, d), batch, seq, 256, *proj_args, cos_x, sin_x)

    pad = jnp.zeros((BLK - N_META, A_Q_W), BF16)
    oa = _win_attn(sink_logits[0].astype(F32), qa, ka, va,
                   jnp.concatenate([pad, ka_m], axis=0), jnp.concatenate([pad, va_m], axis=0), batch, seq)
    ob = _mla_attn(qm, km, vm, km_m, vm_m, min(seq, 1024))

    h1, logits_t = _out_proj(oa, ob, h0, w_out_b[:A_Q_W], w_out_b[A_Q_W:], row2(ln_mix_g[0]), row2(ln_mix_b[0]),
                              wr_hi, wr_lo, 512)

    eidx, wts, rank, counts = _route(logits_t, router_bias[0].astype(F32).reshape(N_EXPERTS, 1), 512)

    counts = counts[:, 0]
    nk = n * TOP_K
    padded = (counts + MOE_BLOCK - 1) // MOE_BLOCK * MOE_BLOCK
    pad_ends = jnp.cumsum(padded)
    pad_starts = pad_ends - padded
    n_rows = (nk + MOE_BLOCK - 1) // MOE_BLOCK * MOE_BLOCK + N_EXPERTS * MOE_BLOCK
    n_blocks = n_rows // MOE_BLOCK
    blk_row0 = jnp.arange(n_blocks, dtype=I32) * MOE_BLOCK
    block_e = jnp.minimum(jnp.sum(pad_ends[None, :] <= blk_row0[:, None], axis=1), N_EXPERTS - 1).astype(I32)
    owner = block_e[:, None] == jnp.arange(N_EXPERTS, dtype=I32)[None, :]
    per_block = lambda table: jnp.sum(jnp.where(owner, table[None, :], 0), axis=1)
    seg_ends = (pad_starts + counts).astype(I32)
    n_used = (pad_ends[-1:] // MOE_BLOCK).astype(I32)
    eid = jnp.arange(N_EXPERTS, dtype=I32)
    later_used = jnp.where((eid[None, :] > eid[:, None]) & (counts[None, :] > 0), eid[None, :], N_EXPERTS)
    next_used = jnp.min(later_used, axis=1)
    next_e = per_block(jnp.where(next_used < N_EXPERTS, next_used, -1).astype(I32)).astype(I32)
    land_slot = (per_block(jnp.cumsum((counts > 0).astype(I32)).astype(I32)) % 2).astype(I32)

    pos3 = _positions(eidx, rank, pad_starts.astype(I32).reshape(N_EXPERTS, 1), 512)
    xs = _dispatch(seg_ends, (padded - counts).astype(I32), n_used, pos3, h1, n_rows, 256)
    ys = _moe(block_e, n_used, next_e, land_slot, xs, w_exp_gate.reshape(w_exp_gate.shape[1:]),
              w_exp_up.reshape(w_exp_up.shape[1:]), w_exp_down.reshape(w_exp_down.shape[1:]))
    wts3 = jnp.broadcast_to(wts.T[:, :, None], (n, TOP_K, BLK))
    out = _combine(pos3, h1, wts3, ys, w_sh_gate[0].astype(BF16), w_sh_up[0].astype(BF16),
                   w_sh_down[0].astype(BF16), row2(ln_ffn_g[0]), row2(ln_ffn_b[0]), 128)
    return out.reshape(batch, seq, d)
```
